```python
import math
import jax, jax.numpy as jnp
from jax import lax
import numpy as np

D_MODEL = 2048
BATCH = 8
SEQ = 8192
DEPTH = 4

GRID_W = 64
CTX_LEN = 256
N_MIXERS = 2
SSM_EXPAND = 2
D_INNER = SSM_EXPAND * D_MODEL
SSM_HEAD_DIM = 64
SSM_HEADS = D_INNER // SSM_HEAD_DIM
SSM_GROUPS = 8
SSM_STATE = 128
SSM_CONV = 5
SSM_CHUNK = 128
XBC_DIM = D_INNER + 2 * SSM_GROUPS * SSM_STATE
SSM_IN_DIM = D_INNER + XBC_DIM + 2 * SSM_HEADS
CONF_KERNEL = 31
D_FF = 4 * D_MODEL
EPS = 1e-6

kernel_name = "hybrid_ssd_conformer_dit_trunk"


def rmsnorm(x, g):
    xf = x.astype(jnp.float32)
    y = xf * lax.rsqrt(jnp.mean(xf * xf, axis=-1, keepdims=True) + EPS)
    return (y * g.astype(jnp.float32)).astype(x.dtype)


def layernorm(x, g, b):
    xf = x.astype(jnp.float32)
    mu = jnp.mean(xf, axis=-1, keepdims=True)
    xc = xf - mu
    var = jnp.mean(xc * xc, axis=-1, keepdims=True)
    y = xc * lax.rsqrt(var + EPS) * g.astype(jnp.float32) + b.astype(jnp.float32)
    return y.astype(x.dtype)


def modulate(h, shift, scale):
    return h * (1 + scale) + shift


def dwconv(u, w, b, seg_len):
    bsz, l, ch = u.shape
    k = w.shape[0]
    us = u.reshape(bsz * (l // seg_len), seg_len, ch)
    y = lax.conv_general_dilated(
        us, w[:, None, :].astype(u.dtype), window_strides=(1,),
        padding=[(k // 2, k // 2)], dimension_numbers=("NWC", "WIO", "NWC"),
        feature_group_count=ch)
    return y.reshape(bsz, l, ch) + b.astype(u.dtype)


def to_scan_order(u, rows, col_major):
    if not col_major:
        return u
    b, l, d = u.shape
    return u.reshape(b, rows, GRID_W, d).swapaxes(1, 2).reshape(b, l, d)


def from_scan_order(u, rows, col_major):
    if not col_major:
        return u
    b, l, d = u.shape
    return u.reshape(b, GRID_W, rows, d).swapaxes(1, 2).reshape(b, l, d)


def ssd_scan(xh, dt, a, bm, cm, h0):
    b, l, h, p = xh.shape
    g, n = bm.shape[-2:]
    q = SSM_CHUNK
    nc = l // q
    hg = h // g
    f32 = jnp.float32
    x_c = xh.astype(f32).reshape(b, nc, q, g, hg, p)
    dt_c = dt.reshape(b, nc, q, g, hg)
    b_c = bm.astype(f32).reshape(b, nc, q, g, n)
    c_c = cm.astype(f32).reshape(b, nc, q, g, n)
    xdt = x_c * dt_c[..., None]
    cs = jnp.cumsum(dt_c * a.reshape(g, hg), axis=2)
    mask = jnp.tril(jnp.ones((q, q), dtype=bool))[None, None, :, :, None, None]
    seg = cs[:, :, :, None] - cs[:, :, None, :]
    decay = jnp.exp(jnp.where(mask, seg, -jnp.inf))
    scores = jnp.einsum("bclgn,bcsgn->bclsg", c_c, b_c)
    y_diag = jnp.einsum("bclsgk,bcsgkp->bclgkp", scores[..., None] * decay, xdt)
    decay_to_end = jnp.exp(cs[:, :, -1:] - cs)
    states = jnp.einsum("bcsgn,bcsgkp->bcgkpn", b_c, xdt * decay_to_end[..., None])
    chunk_decay = jnp.exp(cs[:, :, -1])

    def step(hc, inp):
        st, dec = inp
        return dec[..., None, None] * hc + st, hc

    h_final, h_in = lax.scan(
        step, h0.astype(f32).reshape(b, g, hg, p, n),
        (jnp.moveaxis(states, 1, 0), jnp.moveaxis(chunk_decay, 1, 0)))
    h_in = jnp.moveaxis(h_in, 0, 1)
    y_off = jnp.einsum("bclgn,bcgkpn->bclgkp", c_c, h_in) * jnp.exp(cs)[..., None]
    y = (y_diag + y_off).reshape(b, l, h, p)
    return y, h_final.reshape(b, h, p, n)


def ssd_direction(xs, dt_raw, bm, cm, a_log, dt_bias, d_skip, h0, reverse):
    if reverse:
        xs, dt_raw, bm, cm = (jnp.flip(t, axis=1) for t in (xs, dt_raw, bm, cm))
    dt = jax.nn.softplus(dt_raw.astype(jnp.float32) + dt_bias.astype(jnp.float32))
    a = -jnp.exp(a_log.astype(jnp.float32))
    y, hf = ssd_scan(xs, dt, a, bm, cm, h0)
    y = y + d_skip.astype(jnp.float32)[:, None] * xs.astype(jnp.float32)
    if reverse:
        y = jnp.flip(y, axis=1)
    return y, hf


def mamba_mix(u, w_in, conv_w, conv_b, a_log_f, dt_bias_f, d_f, a_log_b, dt_bias_b,
              d_b, norm_g, w_out, h0_f, h0_b, with_output=True):
    b, l, _ = u.shape
    proj = u @ w_in
    z = proj[..., :D_INNER]
    xbc = proj[..., D_INNER:D_INNER + XBC_DIM]
    dt_raw = proj[..., D_INNER + XBC_DIM:]
    xbc = jax.nn.silu(dwconv(xbc, conv_w, conv_b, l))
    gn = SSM_GROUPS * SSM_STATE
    xs = xbc[..., :D_INNER].reshape(b, l, SSM_HEADS, SSM_HEAD_DIM)
    bm = xbc[..., D_INNER:D_INNER + gn].reshape(b, l, SSM_GROUPS, SSM_STATE)
    cm = xbc[..., D_INNER + gn:].reshape(b, l, SSM_GROUPS, SSM_STATE)
    y_f, hf = ssd_direction(xs, dt_raw[..., :SSM_HEADS], bm, cm, a_log_f, dt_bias_f, d_f,
                            h0_f, reverse=False)
    y_b, hb = ssd_direction(xs, dt_raw[..., SSM_HEADS:], bm, cm, a_log_b, dt_bias_b, d_b,
                            h0_b, reverse=True)
    if not with_output:
        return None, hf, hb
    y = (y_f + y_b).reshape(b, l, D_INNER)
    y = rmsnorm(y * jax.nn.silu(z.astype(jnp.float32)), norm_g)
    return y.astype(u.dtype) @ w_out, hf, hb


def conformer_mix(u, pw1_w, pw1_b, dw_w, dw_b, ln_g, ln_b, pw2_w, pw2_b, seg_len):
    a = u @ pw1_w + pw1_b
    v = a[..., :D_MODEL] * jax.nn.sigmoid(a[..., D_MODEL:])
    v = dwconv(v, dw_w, dw_b, seg_len)
    v = layernorm(v, ln_g, ln_b)
    v = v * jax.nn.sigmoid(v)
    return v @ pw2_w + pw2_b


def sq_relu_mlp(u, w1, w2):
    return jnp.square(jax.nn.relu(u @ w1)) @ w2


def _fwd_setup_inputs(seed: int = 0) -> dict:
    key = jax.random.key(seed)
    ks = jax.random.split(key, 40)
    f32 = jnp.float32
    n_ssm = (DEPTH + 1) // 2
    n_conf = DEPTH // 2

    def nrm(k, shape, scale):
        return jax.random.normal(k, shape, f32) * scale

    def gain(k, shape):
        return 1.0 + 0.05 * jax.random.normal(k, shape, f32)

    def dt_bias(k):
        dt = jnp.exp(jax.random.uniform(k, (n_ssm, SSM_HEADS), f32,
                                        math.log(1e-3), math.log(1e-1)))
        return dt + jnp.log(-jnp.expm1(-dt))

    return {
        "x": nrm(ks[0], (BATCH, SEQ, D_MODEL), 1.0),
        "c": nrm(ks[1], (BATCH, D_MODEL), 1.0),
        "ctx": nrm(ks[2], (BATCH, CTX_LEN, D_MODEL), 1.0),
        "c_ctx": nrm(ks[3], (D_MODEL,), 1.0),
        "mod_w": nrm(ks[4], (DEPTH, D_MODEL, 6 * D_MODEL), 0.5 * D_MODEL ** -0.5),
        "mod_b": nrm(ks[5], (DEPTH, 6 * D_MODEL), 0.02),
        "pre_mix_g": gain(ks[6], (DEPTH, D_MODEL)),
        "post_mix_g": gain(ks[7], (DEPTH, D_MODEL)),
        "pre_mlp_g": gain(ks[8], (DEPTH, D_MODEL)),
        "post_mlp_g": gain(ks[9], (DEPTH, D_MODEL)),
        "mlp_w1": nrm(ks[10], (DEPTH, D_MODEL, D_FF), D_MODEL ** -0.5),
        "mlp_w2": nrm(ks[11], (DEPTH, D_FF, D_MODEL), D_FF ** -0.5),
        "ssm_in_w": nrm(ks[12], (n_ssm, D_MODEL, SSM_IN_DIM), D_MODEL ** -0.5),
        "ssm_conv_w": nrm(ks[13], (n_ssm, SSM_CONV, XBC_DIM), SSM_CONV ** -0.5),
        "ssm_conv_b": nrm(ks[14], (n_ssm, XBC_DIM), 0.02),
        "ssm_a_log_f": jnp.log(jax.random.uniform(ks[15], (n_ssm, SSM_HEADS), f32, 1.0, 16.0)),
        "ssm_dt_bias_f": dt_bias(ks[16]),
        "ssm_d_f": gain(ks[17], (n_ssm, SSM_HEADS)),
        "ssm_a_log_b": jnp.log(jax.random.uniform(ks[18], (n_ssm, SSM_HEADS), f32, 1.0, 16.0)),
        "ssm_dt_bias_b": dt_bias(ks[19]),
        "ssm_d_b": gain(ks[20], (n_ssm, SSM_HEADS)),
        "ssm_norm_g": gain(ks[21], (n_ssm, D_INNER)),
        "ssm_out_w": nrm(ks[22], (n_ssm, D_INNER, D_MODEL), D_INNER ** -0.5),
        "conf_pw1_w": nrm(ks[23], (n_conf, D_MODEL, 2 * D_MODEL), D_MODEL ** -0.5),
        "conf_pw1_b": nrm(ks[24], (n_conf, 2 * D_MODEL), 0.02),
        "conf_dw_w": nrm(ks[25], (n_conf, CONF_KERNEL, D_MODEL), CONF_KERNEL ** -0.5),
        "conf_dw_b": nrm(ks[26], (n_conf, D_MODEL), 0.02),
        "conf_ln_g": gain(ks[27], (n_conf, D_MODEL)),
        "conf_ln_b": nrm(ks[28], (n_conf, D_MODEL), 0.02),
        "conf_pw2_w": nrm(ks[29], (n_conf, D_MODEL, D_MODEL), D_MODEL ** -0.5),
        "conf_pw2_b": nrm(ks[30], (n_conf, D_MODEL), 0.02),
    }


def _fwd_reference(x, c, ctx, c_ctx, mod_w, mod_b, pre_mix_g, post_mix_g, pre_mlp_g, post_mlp_g,
              mlp_w1, mlp_w2, ssm_in_w, ssm_conv_w, ssm_conv_b, ssm_a_log_f, ssm_dt_bias_f,
              ssm_d_f, ssm_a_log_b, ssm_dt_bias_b, ssm_d_b, ssm_norm_g, ssm_out_w,
              conf_pw1_w, conf_pw1_b, conf_dw_w, conf_dw_b, conf_ln_g, conf_ln_b,
              conf_pw2_w, conf_pw2_b):
    bsz, seq_len, _ = x.shape
    rows = seq_len // GRID_W
    sc = jax.nn.silu(c)
    scc = jax.nn.silu(c_ctx)
    h, hc = x, ctx
    for i in range(DEPTH):
        last = i == DEPTH - 1
        kind = i % N_MIXERS
        j = i // N_MIXERS
        col_major = (j % 2) == 1
        m = sc @ mod_w[i] + mod_b[i]
        sh1, sc1, g1, sh2, sc2, g2 = jnp.split(m[:, None, :], 6, axis=-1)
        mc = scc @ mod_w[i] + mod_b[i]
        csh1, csc1, cg1, csh2, csc2, cg2 = jnp.split(mc, 6)

        u = modulate(rmsnorm(h, pre_mix_g[i]), sh1, sc1)
        uc = modulate(rmsnorm(hc, pre_mix_g[i]), csh1, csc1)
        u = to_scan_order(u, rows, col_major)
        if kind == 0:
            p = (ssm_in_w[j], ssm_conv_w[j], ssm_conv_b[j], ssm_a_log_f[j], ssm_dt_bias_f[j],
                 ssm_d_f[j], ssm_a_log_b[j], ssm_dt_bias_b[j], ssm_d_b[j], ssm_norm_g[j],
                 ssm_out_w[j])
            zeros = jnp.zeros((bsz, SSM_HEADS, SSM_HEAD_DIM, SSM_STATE), jnp.float32)
            yc, s_f, s_b = mamba_mix(uc, *p, zeros, zeros, with_output=not last)
            y, _, _ = mamba_mix(u, *p, s_f, s_b)
        else:
            p = (conf_pw1_w[j], conf_pw1_b[j], conf_dw_w[j], conf_dw_b[j], conf_ln_g[j],
                 conf_ln_b[j], conf_pw2_w[j], conf_pw2_b[j])
            seg = rows if col_major else GRID_W
            y = conformer_mix(u, *p, seg)
            yc = None if last else conformer_mix(uc, *p, uc.shape[1])
        y = from_scan_order(y, rows, col_major)
        h = h + (g1 * rmsnorm(y, post_mix_g[i])).astype(h.dtype)

        f = sq_relu_mlp(modulate(rmsnorm(h, pre_mlp_g[i]), sh2, sc2), mlp_w1[i], mlp_w2[i])
        h = h + (g2 * rmsnorm(f, post_mlp_g[i])).astype(h.dtype)

        if not last:
            hc = hc + (cg1 * rmsnorm(yc, post_mix_g[i])).astype(hc.dtype)
            fc = sq_relu_mlp(modulate(rmsnorm(hc, pre_mlp_g[i]), csh2, csc2),
                             mlp_w1[i], mlp_w2[i])
            hc = hc + (cg2 * rmsnorm(fc, post_mlp_g[i])).astype(hc.dtype)
    return h


import jax as _jax
import jax.numpy as _jnp

TWIN_FORMAT = 'train_step'
FWD_PARAMS = ['x', 'c', 'ctx', 'c_ctx', 'mod_w', 'mod_b', 'pre_mix_g', 'post_mix_g', 'pre_mlp_g', 'post_mlp_g', 'mlp_w1', 'mlp_w2', 'ssm_in_w', 'ssm_conv_w', 'ssm_conv_b', 'ssm_a_log_f', 'ssm_dt_bias_f', 'ssm_d_f', 'ssm_a_log_b', 'ssm_dt_bias_b', 'ssm_d_b', 'ssm_norm_g', 'ssm_out_w', 'conf_pw1_w', 'conf_pw1_b', 'conf_dw_w', 'conf_dw_b', 'conf_ln_g', 'conf_ln_b', 'conf_pw2_w', 'conf_pw2_b']
TWIN_WEIGHTS = ['c_ctx', 'mod_w', 'mod_b', 'pre_mix_g', 'post_mix_g', 'pre_mlp_g', 'post_mlp_g', 'mlp_w1', 'mlp_w2', 'ssm_in_w', 'ssm_conv_w', 'ssm_conv_b', 'ssm_a_log_f', 'ssm_dt_bias_f', 'ssm_d_f', 'ssm_a_log_b', 'ssm_dt_bias_b', 'ssm_d_b', 'ssm_norm_g', 'ssm_out_w', 'conf_pw1_w', 'conf_pw1_b', 'conf_dw_w', 'conf_dw_b', 'conf_ln_g', 'conf_ln_b', 'conf_pw2_w', 'conf_pw2_b']
TWIN_DIFF_INPUT = 'x'
TWIN_INPUTS = ['x', 'c', 'ctx', 'c_ctx', 'mod_w', 'mod_b', 'pre_mix_g', 'post_mix_g', 'pre_mlp_g', 'post_mlp_g', 'mlp_w1', 'mlp_w2', 'ssm_in_w', 'ssm_conv_w', 'ssm_conv_b', 'ssm_a_log_f', 'ssm_dt_bias_f', 'ssm_d_f', 'ssm_a_log_b', 'ssm_dt_bias_b', 'ssm_d_b', 'ssm_norm_g', 'ssm_out_w', 'conf_pw1_w', 'conf_pw1_b', 'conf_dw_w', 'conf_dw_b', 'conf_ln_g', 'conf_ln_b', 'conf_pw2_w', 'conf_pw2_b', 'loss_target', 'm_c_ctx', 'm_mod_w', 'm_mod_b', 'm_pre_mix_g', 'm_post_mix_g', 'm_pre_mlp_g', 'm_post_mlp_g', 'm_mlp_w1', 'm_mlp_w2', 'm_ssm_in_w', 'm_ssm_conv_w', 'm_ssm_conv_b', 'm_ssm_a_log_f', 'm_ssm_dt_bias_f', 'm_ssm_d_f', 'm_ssm_a_log_b', 'm_ssm_dt_bias_b', 'm_ssm_d_b', 'm_ssm_norm_g', 'm_ssm_out_w', 'm_conf_pw1_w', 'm_conf_pw1_b', 'm_conf_dw_w', 'm_conf_dw_b', 'm_conf_ln_g', 'm_conf_ln_b', 'm_conf_pw2_w', 'm_conf_pw2_b', 'v_c_ctx', 'v_mod_w', 'v_mod_b', 'v_pre_mix_g', 'v_post_mix_g', 'v_pre_mlp_g', 'v_post_mlp_g', 'v_mlp_w1', 'v_mlp_w2', 'v_ssm_in_w', 'v_ssm_conv_w', 'v_ssm_conv_b', 'v_ssm_a_log_f', 'v_ssm_dt_bias_f', 'v_ssm_d_f', 'v_ssm_a_log_b', 'v_ssm_dt_bias_b', 'v_ssm_d_b', 'v_ssm_norm_g', 'v_ssm_out_w', 'v_conf_pw1_w', 'v_conf_pw1_b', 'v_conf_dw_w', 'v_conf_dw_b', 'v_conf_ln_g', 'v_conf_ln_b', 'v_conf_pw2_w', 'v_conf_pw2_b']
TWIN_OUTPUTS = ['loss', 'grad_x', 'grad_c_ctx', 'grad_mod_w', 'grad_mod_b', 'grad_pre_mix_g', 'grad_post_mix_g', 'grad_pre_mlp_g', 'grad_post_mlp_g', 'grad_mlp_w1', 'grad_mlp_w2', 'grad_ssm_in_w', 'grad_ssm_conv_w', 'grad_ssm_conv_b', 'grad_ssm_a_log_f', 'grad_ssm_dt_bias_f', 'grad_ssm_d_f', 'grad_ssm_a_log_b', 'grad_ssm_dt_bias_b', 'grad_ssm_d_b', 'grad_ssm_norm_g', 'grad_ssm_out_w', 'grad_conf_pw1_w', 'grad_conf_pw1_b', 'grad_conf_dw_w', 'grad_conf_dw_b', 'grad_conf_ln_g', 'grad_conf_ln_b', 'grad_conf_pw2_w', 'grad_conf_pw2_b', 'delta_c_ctx', 'delta_mod_w', 'delta_mod_b', 'delta_pre_mix_g', 'delta_post_mix_g', 'delta_pre_mlp_g', 'delta_post_mlp_g', 'delta_mlp_w1', 'delta_mlp_w2', 'delta_ssm_in_w', 'delta_ssm_conv_w', 'delta_ssm_conv_b', 'delta_ssm_a_log_f', 'delta_ssm_dt_bias_f', 'delta_ssm_d_f', 'delta_ssm_a_log_b', 'delta_ssm_dt_bias_b', 'delta_ssm_d_b', 'delta_ssm_norm_g', 'delta_ssm_out_w', 'delta_conf_pw1_w', 'delta_conf_pw1_b', 'delta_conf_dw_w', 'delta_conf_dw_b', 'delta_conf_ln_g', 'delta_conf_ln_b', 'delta_conf_pw2_w', 'delta_conf_pw2_b', 'new_m_c_ctx', 'new_m_mod_w', 'new_m_mod_b', 'new_m_pre_mix_g', 'new_m_post_mix_g', 'new_m_pre_mlp_g', 'new_m_post_mlp_g', 'new_m_mlp_w1', 'new_m_mlp_w2', 'new_m_ssm_in_w', 'new_m_ssm_conv_w', 'new_m_ssm_conv_b', 'new_m_ssm_a_log_f', 'new_m_ssm_dt_bias_f', 'new_m_ssm_d_f', 'new_m_ssm_a_log_b', 'new_m_ssm_dt_bias_b', 'new_m_ssm_d_b', 'new_m_ssm_norm_g', 'new_m_ssm_out_w', 'new_m_conf_pw1_w', 'new_m_conf_pw1_b', 'new_m_conf_dw_w', 'new_m_conf_dw_b', 'new_m_conf_ln_g', 'new_m_conf_ln_b', 'new_m_conf_pw2_w', 'new_m_conf_pw2_b', 'new_v_c_ctx', 'new_v_mod_w', 'new_v_mod_b', 'new_v_pre_mix_g', 'new_v_post_mix_g', 'new_v_pre_mlp_g', 'new_v_post_mlp_g', 'new_v_mlp_w1', 'new_v_mlp_w2', 'new_v_ssm_in_w', 'new_v_ssm_conv_w', 'new_v_ssm_conv_b', 'new_v_ssm_a_log_f', 'new_v_ssm_dt_bias_f', 'new_v_ssm_d_f', 'new_v_ssm_a_log_b', 'new_v_ssm_dt_bias_b', 'new_v_ssm_d_b', 'new_v_ssm_norm_g', 'new_v_ssm_out_w', 'new_v_conf_pw1_w', 'new_v_conf_pw1_b', 'new_v_conf_dw_w', 'new_v_conf_dw_b', 'new_v_conf_ln_g', 'new_v_conf_ln_b', 'new_v_conf_pw2_w', 'new_v_conf_pw2_b']
TWIN_LEAF_KINDS = {'loss': 'loss', 'grad_x': 'grad_x', 'grad_c_ctx': 'grad_w', 'grad_mod_w': 'grad_w', 'grad_mod_b': 'grad_w', 'grad_pre_mix_g': 'grad_w', 'grad_post_mix_g': 'grad_w', 'grad_pre_mlp_g': 'grad_w', 'grad_post_mlp_g': 'grad_w', 'grad_mlp_w1': 'grad_w', 'grad_mlp_w2': 'grad_w', 'grad_ssm_in_w': 'grad_w', 'grad_ssm_conv_w': 'grad_w', 'grad_ssm_conv_b': 'grad_w', 'grad_ssm_a_log_f': 'grad_w', 'grad_ssm_dt_bias_f': 'grad_w', 'grad_ssm_d_f': 'grad_w', 'grad_ssm_a_log_b': 'grad_w', 'grad_ssm_dt_bias_b': 'grad_w', 'grad_ssm_d_b': 'grad_w', 'grad_ssm_norm_g': 'grad_w', 'grad_ssm_out_w': 'grad_w', 'grad_conf_pw1_w': 'grad_w', 'grad_conf_pw1_b': 'grad_w', 'grad_conf_dw_w': 'grad_w', 'grad_conf_dw_b': 'grad_w', 'grad_conf_ln_g': 'grad_w', 'grad_conf_ln_b': 'grad_w', 'grad_conf_pw2_w': 'grad_w', 'grad_conf_pw2_b': 'grad_w', 'delta_c_ctx': 'delta_w', 'delta_mod_w': 'delta_w', 'delta_mod_b': 'delta_w', 'delta_pre_mix_g': 'delta_w', 'delta_post_mix_g': 'delta_w', 'delta_pre_mlp_g': 'delta_w', 'delta_post_mlp_g': 'delta_w', 'delta_mlp_w1': 'delta_w', 'delta_mlp_w2': 'delta_w', 'delta_ssm_in_w': 'delta_w', 'delta_ssm_conv_w': 'delta_w', 'delta_ssm_conv_b': 'delta_w', 'delta_ssm_a_log_f': 'delta_w', 'delta_ssm_dt_bias_f': 'delta_w', 'delta_ssm_d_f': 'delta_w', 'delta_ssm_a_log_b': 'delta_w', 'delta_ssm_dt_bias_b': 'delta_w', 'delta_ssm_d_b': 'delta_w', 'delta_ssm_norm_g': 'delta_w', 'delta_ssm_out_w': 'delta_w', 'delta_conf_pw1_w': 'delta_w', 'delta_conf_pw1_b': 'delta_w', 'delta_conf_dw_w': 'delta_w', 'delta_conf_dw_b': 'delta_w', 'delta_conf_ln_g': 'delta_w', 'delta_conf_ln_b': 'delta_w', 'delta_conf_pw2_w': 'delta_w', 'delta_conf_pw2_b': 'delta_w', 'new_m_c_ctx': 'new_m', 'new_m_mod_w': 'new_m', 'new_m_mod_b': 'new_m', 'new_m_pre_mix_g': 'new_m', 'new_m_post_mix_g': 'new_m', 'new_m_pre_mlp_g': 'new_m', 'new_m_post_mlp_g': 'new_m', 'new_m_mlp_w1': 'new_m', 'new_m_mlp_w2': 'new_m', 'new_m_ssm_in_w': 'new_m', 'new_m_ssm_conv_w': 'new_m', 'new_m_ssm_conv_b': 'new_m', 'new_m_ssm_a_log_f': 'new_m', 'new_m_ssm_dt_bias_f': 'new_m', 'new_m_ssm_d_f': 'new_m', 'new_m_ssm_a_log_b': 'new_m', 'new_m_ssm_dt_bias_b': 'new_m', 'new_m_ssm_d_b': 'new_m', 'new_m_ssm_norm_g': 'new_m', 'new_m_ssm_out_w': 'new_m', 'new_m_conf_pw1_w': 'new_m', 'new_m_conf_pw1_b': 'new_m', 'new_m_conf_dw_w': 'new_m', 'new_m_conf_dw_b': 'new_m', 'new_m_conf_ln_g': 'new_m', 'new_m_conf_ln_b': 'new_m', 'new_m_conf_pw2_w': 'new_m', 'new_m_conf_pw2_b': 'new_m', 'new_v_c_ctx': 'new_v', 'new_v_mod_w': 'new_v', 'new_v_mod_b': 'new_v', 'new_v_pre_mix_g': 'new_v', 'new_v_post_mix_g': 'new_v', 'new_v_pre_mlp_g': 'new_v', 'new_v_post_mlp_g': 'new_v', 'new_v_mlp_w1': 'new_v', 'new_v_mlp_w2': 'new_v', 'new_v_ssm_in_w': 'new_v', 'new_v_ssm_conv_w': 'new_v', 'new_v_ssm_conv_b': 'new_v', 'new_v_ssm_a_log_f': 'new_v', 'new_v_ssm_dt_bias_f': 'new_v', 'new_v_ssm_d_f': 'new_v', 'new_v_ssm_a_log_b': 'new_v', 'new_v_ssm_dt_bias_b': 'new_v', 'new_v_ssm_d_b': 'new_v', 'new_v_ssm_norm_g': 'new_v', 'new_v_ssm_out_w': 'new_v', 'new_v_conf_pw1_w': 'new_v', 'new_v_conf_pw1_b': 'new_v', 'new_v_conf_dw_w': 'new_v', 'new_v_conf_dw_b': 'new_v', 'new_v_conf_ln_g': 'new_v', 'new_v_conf_ln_b': 'new_v', 'new_v_conf_pw2_w': 'new_v', 'new_v_conf_pw2_b': 'new_v'}


def _forward(args):
    return _fwd_reference(*[args[k] for k in FWD_PARAMS])


def _output_shape():
    def fwd():
        inp = _fwd_setup_inputs(0)
        return _fwd_reference(*[inp[k] for k in FWD_PARAMS])
    out = _jax.eval_shape(fwd)
    return out.shape, out.dtype

N_MICROBATCH = 1
ADAM_LR = 0.001
ADAM_B1 = 0.9
ADAM_B2 = 0.999
ADAM_EPS = 1e-08
ADAM_WD = 0.01
ADAM_STEP = 10
PER_EXAMPLE_BATCH_AXIS = {'x': 0, 'c': 0, 'ctx': 0, 'loss_target': 0}
SHARED_INPUTS = []
_WEIGHT_DTYPES = {'c_ctx': _jnp.float32, 'mod_w': _jnp.float32, 'mod_b': _jnp.float32, 'pre_mix_g': _jnp.float32, 'post_mix_g': _jnp.float32, 'pre_mlp_g': _jnp.float32, 'post_mlp_g': _jnp.float32, 'mlp_w1': _jnp.float32, 'mlp_w2': _jnp.float32, 'ssm_in_w': _jnp.float32, 'ssm_conv_w': _jnp.float32, 'ssm_conv_b': _jnp.float32, 'ssm_a_log_f': _jnp.float32, 'ssm_dt_bias_f': _jnp.float32, 'ssm_d_f': _jnp.float32, 'ssm_a_log_b': _jnp.float32, 'ssm_dt_bias_b': _jnp.float32, 'ssm_d_b': _jnp.float32, 'ssm_norm_g': _jnp.float32, 'ssm_out_w': _jnp.float32, 'conf_pw1_w': _jnp.float32, 'conf_pw1_b': _jnp.float32, 'conf_dw_w': _jnp.float32, 'conf_dw_b': _jnp.float32, 'conf_ln_g': _jnp.float32, 'conf_ln_b': _jnp.float32, 'conf_pw2_w': _jnp.float32, 'conf_pw2_b': _jnp.float32}
MOMENT_SCALE = {'c_ctx': 1.177211e-02, 'mod_w': 1.380932e+00, 'mod_b': 2.918433e+00, 'pre_mix_g': 1.407661e-01, 'post_mix_g': 3.303851e+00, 'pre_mlp_g': 1.505322e-01, 'post_mlp_g': 3.408116e+00, 'mlp_w1': 1.257972e-01, 'mlp_w2': 5.984582e-01, 'ssm_in_w': 1.127417e-01, 'ssm_conv_w': 1.396657e-01, 'ssm_conv_b': 3.295079e-01, 'ssm_a_log_f': 5.956062e-01, 'ssm_dt_bias_f': 1.315396e-01, 'ssm_d_f': 3.336894e-01, 'ssm_a_log_b': 6.275363e-01, 'ssm_dt_bias_b': 1.200133e-01, 'ssm_d_b': 3.336894e-01, 'ssm_norm_g': 2.390200e-01, 'ssm_out_w': 3.386834e-01, 'conf_pw1_w': 1.800652e-01, 'conf_pw1_b': 6.451683e-01, 'conf_dw_w': 2.582180e-01, 'conf_dw_b': 1.101535e+00, 'conf_ln_g': 7.940385e-01, 'conf_ln_b': 1.099487e+00, 'conf_pw2_w': 4.994213e-01, 'conf_pw2_b': 2.227325e+00}


def _to_microbatches(a, axis):
    t = _jnp.moveaxis(a, axis, 0)
    t = t.reshape((N_MICROBATCH, t.shape[0] // N_MICROBATCH) + t.shape[1:])
    return _jnp.moveaxis(t, 1, axis + 1)


def setup_inputs(seed: int = 0) -> dict:
    inp = _fwd_setup_inputs(seed)
    key = _jax.random.fold_in(_jax.random.key(seed), 7919)
    shape, _ = _output_shape()
    out = dict(inp)
    out["loss_target"] = _jax.random.normal(_jax.random.fold_in(key, 0), shape, _jnp.float32)
    for i, name in enumerate(TWIN_WEIGHTS):
        w = inp[name].astype(_jnp.float32)
        if MOMENT_SCALE is None:
            s = _jnp.sqrt(_jnp.mean(_jnp.square(w)) + 1e-30)
        else:
            s = MOMENT_SCALE[name]
        km, kv = _jax.random.split(_jax.random.fold_in(key, i + 1))
        out[name] = w
        out["m_" + name] = s * _jax.random.normal(km, w.shape, _jnp.float32)
        out["v_" + name] = (s * s) * _jax.random.uniform(kv, w.shape, _jnp.float32, 0.5, 1.5)
    if N_MICROBATCH > 1:
        for name, axis in PER_EXAMPLE_BATCH_AXIS.items():
            out[name] = _to_microbatches(out[name], axis)
    return {'x': out['x'], 'c': out['c'], 'ctx': out['ctx'], 'c_ctx': out['c_ctx'], 'mod_w': out['mod_w'], 'mod_b': out['mod_b'], 'pre_mix_g': out['pre_mix_g'], 'post_mix_g': out['post_mix_g'], 'pre_mlp_g': out['pre_mlp_g'], 'post_mlp_g': out['post_mlp_g'], 'mlp_w1': out['mlp_w1'], 'mlp_w2': out['mlp_w2'], 'ssm_in_w': out['ssm_in_w'], 'ssm_conv_w': out['ssm_conv_w'], 'ssm_conv_b': out['ssm_conv_b'], 'ssm_a_log_f': out['ssm_a_log_f'], 'ssm_dt_bias_f': out['ssm_dt_bias_f'], 'ssm_d_f': out['ssm_d_f'], 'ssm_a_log_b': out['ssm_a_log_b'], 'ssm_dt_bias_b': out['ssm_dt_bias_b'], 'ssm_d_b': out['ssm_d_b'], 'ssm_norm_g': out['ssm_norm_g'], 'ssm_out_w': out['ssm_out_w'], 'conf_pw1_w': out['conf_pw1_w'], 'conf_pw1_b': out['conf_pw1_b'], 'conf_dw_w': out['conf_dw_w'], 'conf_dw_b': out['conf_dw_b'], 'conf_ln_g': out['conf_ln_g'], 'conf_ln_b': out['conf_ln_b'], 'conf_pw2_w': out['conf_pw2_w'], 'conf_pw2_b': out['conf_pw2_b'], 'loss_target': out['loss_target'], 'm_c_ctx': out['m_c_ctx'], 'm_mod_w': out['m_mod_w'], 'm_mod_b': out['m_mod_b'], 'm_pre_mix_g': out['m_pre_mix_g'], 'm_post_mix_g': out['m_post_mix_g'], 'm_pre_mlp_g': out['m_pre_mlp_g'], 'm_post_mlp_g': out['m_post_mlp_g'], 'm_mlp_w1': out['m_mlp_w1'], 'm_mlp_w2': out['m_mlp_w2'], 'm_ssm_in_w': out['m_ssm_in_w'], 'm_ssm_conv_w': out['m_ssm_conv_w'], 'm_ssm_conv_b': out['m_ssm_conv_b'], 'm_ssm_a_log_f': out['m_ssm_a_log_f'], 'm_ssm_dt_bias_f': out['m_ssm_dt_bias_f'], 'm_ssm_d_f': out['m_ssm_d_f'], 'm_ssm_a_log_b': out['m_ssm_a_log_b'], 'm_ssm_dt_bias_b': out['m_ssm_dt_bias_b'], 'm_ssm_d_b': out['m_ssm_d_b'], 'm_ssm_norm_g': out['m_ssm_norm_g'], 'm_ssm_out_w': out['m_ssm_out_w'], 'm_conf_pw1_w': out['m_conf_pw1_w'], 'm_conf_pw1_b': out['m_conf_pw1_b'], 'm_conf_dw_w': out['m_conf_dw_w'], 'm_conf_dw_b': out['m_conf_dw_b'], 'm_conf_ln_g': out['m_conf_ln_g'], 'm_conf_ln_b': out['m_conf_ln_b'], 'm_conf_pw2_w': out['m_conf_pw2_w'], 'm_conf_pw2_b': out['m_conf_pw2_b'], 'v_c_ctx': out['v_c_ctx'], 'v_mod_w': out['v_mod_w'], 'v_mod_b': out['v_mod_b'], 'v_pre_mix_g': out['v_pre_mix_g'], 'v_post_mix_g': out['v_post_mix_g'], 'v_pre_mlp_g': out['v_pre_mlp_g'], 'v_post_mlp_g': out['v_post_mlp_g'], 'v_mlp_w1': out['v_mlp_w1'], 'v_mlp_w2': out['v_mlp_w2'], 'v_ssm_in_w': out['v_ssm_in_w'], 'v_ssm_conv_w': out['v_ssm_conv_w'], 'v_ssm_conv_b': out['v_ssm_conv_b'], 'v_ssm_a_log_f': out['v_ssm_a_log_f'], 'v_ssm_dt_bias_f': out['v_ssm_dt_bias_f'], 'v_ssm_d_f': out['v_ssm_d_f'], 'v_ssm_a_log_b': out['v_ssm_a_log_b'], 'v_ssm_dt_bias_b': out['v_ssm_dt_bias_b'], 'v_ssm_d_b': out['v_ssm_d_b'], 'v_ssm_norm_g': out['v_ssm_norm_g'], 'v_ssm_out_w': out['v_ssm_out_w'], 'v_conf_pw1_w': out['v_conf_pw1_w'], 'v_conf_pw1_b': out['v_conf_pw1_b'], 'v_conf_dw_w': out['v_conf_dw_w'], 'v_conf_dw_b': out['v_conf_dw_b'], 'v_conf_ln_g': out['v_conf_ln_g'], 'v_conf_ln_b': out['v_conf_ln_b'], 'v_conf_pw2_w': out['v_conf_pw2_w'], 'v_conf_pw2_b': out['v_conf_pw2_b']}


def _loss(weights, diff, rest, loss_target):
    with _jax.named_scope("forward"):
        args = {**rest, TWIN_DIFF_INPUT: diff, **{k: w.astype(_WEIGHT_DTYPES[k]) for k, w in weights.items()}}
        y = _forward(args)
    with _jax.named_scope("loss_head"):
        err = _jnp.square(y.astype(_jnp.float32) - loss_target)
        return 0.5 * _jnp.sum(_jnp.mean(err, axis=-1)) if err.ndim else 0.5 * err


def _adamw(w, g, m, v):
    m = ADAM_B1 * m + (1.0 - ADAM_B1) * g
    v = ADAM_B2 * v + (1.0 - ADAM_B2) * _jnp.square(g)
    m_hat = m / (1.0 - ADAM_B1 ** ADAM_STEP)
    v_hat = v / (1.0 - ADAM_B2 ** ADAM_STEP)
    delta = -ADAM_LR * (m_hat / (_jnp.sqrt(v_hat) + ADAM_EPS) + ADAM_WD * w)
    return delta, m, v


def reference(x, c, ctx, c_ctx, mod_w, mod_b, pre_mix_g, post_mix_g, pre_mlp_g, post_mlp_g, mlp_w1, mlp_w2, ssm_in_w, ssm_conv_w, ssm_conv_b, ssm_a_log_f, ssm_dt_bias_f, ssm_d_f, ssm_a_log_b, ssm_dt_bias_b, ssm_d_b, ssm_norm_g, ssm_out_w, conf_pw1_w, conf_pw1_b, conf_dw_w, conf_dw_b, conf_ln_g, conf_ln_b, conf_pw2_w, conf_pw2_b, loss_target, m_c_ctx, m_mod_w, m_mod_b, m_pre_mix_g, m_post_mix_g, m_pre_mlp_g, m_post_mlp_g, m_mlp_w1, m_mlp_w2, m_ssm_in_w, m_ssm_conv_w, m_ssm_conv_b, m_ssm_a_log_f, m_ssm_dt_bias_f, m_ssm_d_f, m_ssm_a_log_b, m_ssm_dt_bias_b, m_ssm_d_b, m_ssm_norm_g, m_ssm_out_w, m_conf_pw1_w, m_conf_pw1_b, m_conf_dw_w, m_conf_dw_b, m_conf_ln_g, m_conf_ln_b, m_conf_pw2_w, m_conf_pw2_b, v_c_ctx, v_mod_w, v_mod_b, v_pre_mix_g, v_post_mix_g, v_pre_mlp_g, v_post_mlp_g, v_mlp_w1, v_mlp_w2, v_ssm_in_w, v_ssm_conv_w, v_ssm_conv_b, v_ssm_a_log_f, v_ssm_dt_bias_f, v_ssm_d_f, v_ssm_a_log_b, v_ssm_dt_bias_b, v_ssm_d_b, v_ssm_norm_g, v_ssm_out_w, v_conf_pw1_w, v_conf_pw1_b, v_conf_dw_w, v_conf_dw_b, v_conf_ln_g, v_conf_ln_b, v_conf_pw2_w, v_conf_pw2_b):
    given = dict(x=x, c=c, ctx=ctx, c_ctx=c_ctx, mod_w=mod_w, mod_b=mod_b, pre_mix_g=pre_mix_g, post_mix_g=post_mix_g, pre_mlp_g=pre_mlp_g, post_mlp_g=post_mlp_g, mlp_w1=mlp_w1, mlp_w2=mlp_w2, ssm_in_w=ssm_in_w, ssm_conv_w=ssm_conv_w, ssm_conv_b=ssm_conv_b, ssm_a_log_f=ssm_a_log_f, ssm_dt_bias_f=ssm_dt_bias_f, ssm_d_f=ssm_d_f, ssm_a_log_b=ssm_a_log_b, ssm_dt_bias_b=ssm_dt_bias_b, ssm_d_b=ssm_d_b, ssm_norm_g=ssm_norm_g, ssm_out_w=ssm_out_w, conf_pw1_w=conf_pw1_w, conf_pw1_b=conf_pw1_b, conf_dw_w=conf_dw_w, conf_dw_b=conf_dw_b, conf_ln_g=conf_ln_g, conf_ln_b=conf_ln_b, conf_pw2_w=conf_pw2_w, conf_pw2_b=conf_pw2_b, loss_target=loss_target, m_c_ctx=m_c_ctx, m_mod_w=m_mod_w, m_mod_b=m_mod_b, m_pre_mix_g=m_pre_mix_g, m_post_mix_g=m_post_mix_g, m_pre_mlp_g=m_pre_mlp_g, m_post_mlp_g=m_post_mlp_g, m_mlp_w1=m_mlp_w1, m_mlp_w2=m_mlp_w2, m_ssm_in_w=m_ssm_in_w, m_ssm_conv_w=m_ssm_conv_w, m_ssm_conv_b=m_ssm_conv_b, m_ssm_a_log_f=m_ssm_a_log_f, m_ssm_dt_bias_f=m_ssm_dt_bias_f, m_ssm_d_f=m_ssm_d_f, m_ssm_a_log_b=m_ssm_a_log_b, m_ssm_dt_bias_b=m_ssm_dt_bias_b, m_ssm_d_b=m_ssm_d_b, m_ssm_norm_g=m_ssm_norm_g, m_ssm_out_w=m_ssm_out_w, m_conf_pw1_w=m_conf_pw1_w, m_conf_pw1_b=m_conf_pw1_b, m_conf_dw_w=m_conf_dw_w, m_conf_dw_b=m_conf_dw_b, m_conf_ln_g=m_conf_ln_g, m_conf_ln_b=m_conf_ln_b, m_conf_pw2_w=m_conf_pw2_w, m_conf_pw2_b=m_conf_pw2_b, v_c_ctx=v_c_ctx, v_mod_w=v_mod_w, v_mod_b=v_mod_b, v_pre_mix_g=v_pre_mix_g, v_post_mix_g=v_post_mix_g, v_pre_mlp_g=v_pre_mlp_g, v_post_mlp_g=v_post_mlp_g, v_mlp_w1=v_mlp_w1, v_mlp_w2=v_mlp_w2, v_ssm_in_w=v_ssm_in_w, v_ssm_conv_w=v_ssm_conv_w, v_ssm_conv_b=v_ssm_conv_b, v_ssm_a_log_f=v_ssm_a_log_f, v_ssm_dt_bias_f=v_ssm_dt_bias_f, v_ssm_d_f=v_ssm_d_f, v_ssm_a_log_b=v_ssm_a_log_b, v_ssm_dt_bias_b=v_ssm_dt_bias_b, v_ssm_d_b=v_ssm_d_b, v_ssm_norm_g=v_ssm_norm_g, v_ssm_out_w=v_ssm_out_w, v_conf_pw1_w=v_conf_pw1_w, v_conf_pw1_b=v_conf_pw1_b, v_conf_dw_w=v_conf_dw_w, v_conf_dw_b=v_conf_dw_b, v_conf_ln_g=v_conf_ln_g, v_conf_ln_b=v_conf_ln_b, v_conf_pw2_w=v_conf_pw2_w, v_conf_pw2_b=v_conf_pw2_b)
    weights = {n: given[n] for n in TWIN_WEIGHTS}
    shared = {n: given[n] for n in SHARED_INPUTS}
    per_example = {n: given[n] for n in ['x', 'c', 'ctx']}
    grad_fn = _jax.value_and_grad(_loss, argnums=(0, 1))

    def one_microbatch(ex, loss_target):
        ex = dict(ex)
        diff = ex.pop(TWIN_DIFF_INPUT)
        return grad_fn(weights, diff, {**shared, **ex}, loss_target)

    if N_MICROBATCH == 1:
        loss, (grad_w, grad_x) = one_microbatch(per_example, given["loss_target"])
    else:
        def body(carry, xs):
            loss_sum, grad_sum = carry
            l_k, (gw_k, gx_k) = one_microbatch(xs[0], xs[1])
            with _jax.named_scope("update"):
                return (loss_sum + l_k, _jax.tree.map(_jnp.add, grad_sum, gw_k)), gx_k

        init = (_jnp.zeros((), _jnp.float32), _jax.tree.map(_jnp.zeros_like, weights))
        (loss, grad_w), grad_x = _jax.lax.scan(body, init, (per_example, given["loss_target"]))
    with _jax.named_scope("update"):
        delta_w, new_m, new_v = {}, {}, {}
        for n in TWIN_WEIGHTS:
            delta_w[n], new_m[n], new_v[n] = _adamw(weights[n], grad_w[n], given["m_" + n], given["v_" + n])
    return (loss, grad_x, *[grad_w[n] for n in TWIN_WEIGHTS], *[delta_w[n] for n in TWIN_WEIGHTS],
            *[new_m[n] for n in TWIN_WEIGHTS], *[new_v[n] for n in TWIN_WEIGHTS])
```

```python
import functools

import jax
import jax.numpy as jnp
from jax import lax
from jax.experimental import pallas as pl
from jax.experimental.pallas import tpu as pltpu

F32, BF16 = jnp.float32, jnp.bfloat16
SDS = jax.ShapeDtypeStruct

GRID_W = 64
HEAD_P = 64
N_GROUPS = 8
N_STATE = 128
SSM_K = 5
CHUNK = 128
CONF_K = 31
EPS = 1e-6
ADAM_LR, ADAM_B1, ADAM_B2, ADAM_EPS, ADAM_WD, ADAM_STEP = 0.001, 0.9, 0.999, 1e-08, 0.01, 10

LANES = 128
SUBLANES = 8
N_DEV = 8
MESH = pl.DeviceIdType.MESH
VMEM_LIMIT = 56 * 2**20


def _cp(sem=None, vmem=VMEM_LIMIT):
    if sem is None:
        return pltpu.CompilerParams(vmem_limit_bytes=vmem)
    return pltpu.CompilerParams(dimension_semantics=sem, vmem_limit_bytes=vmem)


def _pick(dim, cands):
    for c in cands:
        if c <= dim and dim % c == 0:
            return c
    return dim


def _cvjp(fwd, bwd):
    @jax.custom_vjp
    def op(*args):
        return fwd(*args)[0]
    op.defvjp(fwd, bwd)
    return op


def _rowwise(name, fn, R, tm, rows, vecs, outs, vouts=()):
    n_r, n_v, n_o, n_vo = len(rows), len(vecs), len(outs), len(vouts)
    alias_in = [o[4] for o in outs if o[4] is not None]
    n_a = len(alias_in)

    def body(*refs):
        r = refs[:n_r]
        v = refs[n_r:n_r + n_v]
        o = refs[n_r + n_v + n_a:n_r + n_v + n_a + n_o]
        vo = refs[n_r + n_v + n_a + n_o:]
        i = pl.program_id(0)
        ro, vvals = fn(i * tm, [x[...] for x in r], [x[...] for x in v])
        for ref, val in zip(o, ro):
            ref[...] = val.astype(ref.dtype)
        if n_vo:
            @pl.when(i == 0)
            def _():
                for ref in vo:
                    ref[...] = jnp.zeros_like(ref)
            for ref, val in zip(vo, vvals):
                ref[...] += val

    def colmap(cb):
        return lambda i: (i, cb)

    in_specs = [pl.BlockSpec((tm, w), colmap(cb)) for (_, w, cb) in rows]
    in_specs += [pl.BlockSpec(a.shape, lambda i: (0, 0)) for a in vecs]
    in_specs += [pl.BlockSpec(memory_space=pl.ANY)] * n_a
    out_shape, out_specs, aliases = [], [], {}
    ai = 0
    for k, (w, dt, tot, cb, al) in enumerate(outs):
        out_shape.append(SDS((R, tot if tot else w), dt))
        out_specs.append(pl.BlockSpec((tm, w), colmap(cb)))
        if al is not None:
            aliases[n_r + n_v + ai] = k
            ai += 1
    for (vv, w) in vouts:
        out_shape.append(SDS((vv, w), F32))
        out_specs.append(pl.BlockSpec((vv, w), lambda i: (0, 0)))
    res = pl.pallas_call(
        body, grid=(R // tm,), in_specs=in_specs, out_specs=out_specs, out_shape=out_shape,
        input_output_aliases=aliases, name=name, compiler_params=_cp(("arbitrary",)),
    )(*[a for a, _, _ in rows], *vecs, *alias_in)
    return list(res)


def _vjp_fn(f, n_rows, want_rows, want_vecs, add_last=False):
    def fb(row0, tiles, vecs):
        rows = [t.astype(F32) for t in tiles[:n_rows]]
        outs, vjp = jax.vjp(lambda rs, vs: f(row0, rs, vs), rows, [v.astype(F32) for v in vecs])
        cts = [t.astype(F32) for t in tiles[n_rows:n_rows + len(outs)]]
        g_rows, g_vecs = vjp(cts)
        gr = [g for g, w in zip(g_rows, want_rows) if w]
        if add_last:
            gr[0] = gr[0] + tiles[-1].astype(F32)
        return gr, [g for g, w in zip(g_vecs, want_vecs) if w]
    return fb


def _silu(x):
    return x * jax.nn.sigmoid(x)


def _rms(x, g):
    return x * lax.rsqrt(jnp.mean(x * x, axis=-1, keepdims=True) + EPS) * g


def _is_ctx(row0, tm, lc):
    return (row0 + lax.broadcasted_iota(jnp.int32, (tm, 1), 0)) < lc


def _mm_nn(name, a, b3, out_dtype=F32, bias=None, epilogue=None, tm=None):
    M, K = a.shape
    S, _, Ns = b3.shape
    N = S * Ns
    tm = tm or _pick(M, (768, 512, 384, 256, 128, 64, 32, 16, 8))
    tn = _pick(Ns, (1024, 768, 512, 384, 256, 128))
    tk = _pick(K, (2048, 1024, 512, 256, 128))
    nk, npj = K // tk, Ns // tn
    has_bias = bias is not None

    def body(*refs):
        a_ref, b_ref = refs[0], refs[1]
        bias_ref = refs[2] if has_bias else None
        o_ref = refs[2 + has_bias]
        acc_ref = refs[3 + has_bias]
        k = pl.program_id(2)

        @pl.when(k == 0)
        def _():
            acc_ref[...] = jnp.zeros_like(acc_ref)

        acc_ref[...] += jnp.dot(a_ref[...].astype(BF16), b_ref[...].astype(BF16), preferred_element_type=F32)

        @pl.when(k == nk - 1)
        def _():
            acc = acc_ref[...]
            if has_bias:
                acc = acc + bias_ref[...]
            if epilogue is not None:
                acc = epilogue(acc)
            o_ref[...] = acc.astype(o_ref.dtype)

    in_specs = [pl.BlockSpec((tm, tk), lambda i, j, k: (i, k)),
                pl.BlockSpec((None, tk, tn), lambda i, j, k: (j // npj, k, j % npj))]
    args = [a, b3]
    if has_bias:
        in_specs.append(pl.BlockSpec((1, tn), lambda i, j, k: (0, j)))
        args.append(bias)
    return pl.pallas_call(
        body, grid=(M // tm, N // tn, nk), in_specs=in_specs,
        out_specs=pl.BlockSpec((tm, tn), lambda i, j, k: (i, j)),
        out_shape=SDS((M, N), out_dtype), scratch_shapes=[pltpu.VMEM((tm, tn), F32)],
        name=name, compiler_params=_cp(("parallel", "parallel", "arbitrary")),
    )(*args)


def _mm_nt(name, dc, b3, out_dtype=F32, epilogue=None, extra=None):
    M, N = dc.shape
    S, K, Ns = b3.shape
    tm = _pick(M, (768, 512, 384, 256, 128, 64, 32, 16, 8))
    tko = _pick(K, (1024, 512, 256, 128))
    tnc = _pick(Ns, (1024, 768, 512, 384, 256, 128))
    nn, npn = N // tnc, Ns // tnc
    has_extra = extra is not None

    def body(*refs):
        d_ref, b_ref = refs[0], refs[1]
        e_ref = refs[2] if has_extra else None
        o_ref = refs[2 + has_extra]
        acc_ref = refs[3 + has_extra]
        n = pl.program_id(2)

        @pl.when(n == 0)
        def _():
            acc_ref[...] = jnp.zeros_like(acc_ref)

        acc_ref[...] += lax.dot_general(d_ref[...].astype(BF16), b_ref[...].astype(BF16),
                                        (((1,), (1,)), ((), ())), preferred_element_type=F32)

        @pl.when(n == nn - 1)
        def _():
            acc = acc_ref[...]
            if epilogue is not None:
                acc = epilogue(acc, e_ref[...]) if has_extra else epilogue(acc)
            o_ref[...] = acc.astype(o_ref.dtype)

    in_specs = [pl.BlockSpec((tm, tnc), lambda i, j, n: (i, n)),
                pl.BlockSpec((None, tko, tnc), lambda i, j, n: (n // npn, j, n % npn))]
    args = [dc, b3]
    if has_extra:
        in_specs.append(pl.BlockSpec((tm, tko), lambda i, j, n: (i, j)))
        args.append(extra)
    return pl.pallas_call(
        body, grid=(M // tm, K // tko, nn), in_specs=in_specs,
        out_specs=pl.BlockSpec((tm, tko), lambda i, j, n: (i, j)),
        out_shape=SDS((M, K), out_dtype), scratch_shapes=[pltpu.VMEM((tm, tko), F32)],
        name=name, compiler_params=_cp(("parallel", "parallel", "arbitrary")),
    )(*args)


def _mm_tn(name, a, dc, S, out_dtype=BF16):
    M, K = a.shape
    _, N = dc.shape
    Ns = N // S
    tko = _pick(K, (1024, 512, 256, 128))
    tn = _pick(Ns, (1024, 768, 512, 384, 256, 128))
    tmc = _pick(M, (768, 512, 384, 256, 128, 64, 32, 16, 8))
    nm, npj = M // tmc, Ns // tn

    def body(a_ref, d_ref, o_ref, acc_ref):
        m = pl.program_id(2)

        @pl.when(m == 0)
        def _():
            acc_ref[...] = jnp.zeros_like(acc_ref)

        acc_ref[...] += lax.dot_general(a_ref[...].astype(BF16), d_ref[...].astype(BF16),
                                        (((0,), (0,)), ((), ())), preferred_element_type=F32)

        @pl.when(m == nm - 1)
        def _():
            o_ref[...] = acc_ref[...].astype(o_ref.dtype)

    return pl.pallas_call(
        body, grid=(K // tko, N // tn, nm),
        in_specs=[pl.BlockSpec((tmc, tko), lambda i, j, m: (m, i)),
                  pl.BlockSpec((tmc, tn), lambda i, j, m: (m, j))],
        out_specs=pl.BlockSpec((None, tko, tn), lambda i, j, m: (j // npj, i, j % npj)),
        out_shape=SDS((S, K, Ns), out_dtype), scratch_shapes=[pltpu.VMEM((tko, tn), F32)],
        name=name, compiler_params=_cp(("parallel", "parallel", "arbitrary")),
    )(a, dc)


def _my_pos():
    return lax.axis_index("x"), lax.axis_index("y"), lax.axis_index("c")


def _flip(pos, k):
    x, y, c = pos
    return (1 - x if k & 4 else x, 1 - y if k & 2 else y, 1 - c if k & 1 else c)


def _lin(pos):
    return 4 * pos[0] + 2 * pos[1] + pos[2]


def _all_gather(name, x):
    r, c = x.shape

    def body(x_ref, out_ref, send_sems, recv_sems, local_sem):
        me = _my_pos()
        sibling = _flip(me, 1)
        chips = [2, 4, 6]

        def copy(k, block, to, src=None):
            dst = out_ref.at[_lin(block)]
            return pltpu.make_async_remote_copy(
                src_ref=dst if src is None else src, dst_ref=dst,
                send_sem=send_sems.at[k], recv_sem=recv_sems.at[k], device_id=to, device_id_type=MESH)

        mine = pltpu.make_async_copy(x_ref, out_ref.at[_lin(me)], local_sem)
        mine.start()
        first = [copy(0, me, sibling, src=x_ref)]
        first += [copy(1 + j, me, _flip(me, m), src=x_ref) for j, m in enumerate(chips)]
        for cp in first:
            cp.start()
        passed = [copy(4 + j, _flip(me, m), sibling) for j, m in enumerate(chips)]
        for j, m in enumerate(chips):
            copy(1 + j, _flip(me, m), me).wait_recv()
            passed[j].start()
        copy(0, sibling, me).wait_recv()
        for j, m in enumerate(chips):
            copy(4 + j, _flip(sibling, m), me).wait_recv()
        for cp in first + passed:
            cp.wait_send()
        mine.wait()

    return pl.pallas_call(
        body, out_shape=SDS((N_DEV, r, c), x.dtype),
        in_specs=[pl.BlockSpec(memory_space=pl.ANY)], out_specs=pl.BlockSpec(memory_space=pl.ANY),
        scratch_shapes=[pltpu.SemaphoreType.DMA((7,)), pltpu.SemaphoreType.DMA((7,)), pltpu.SemaphoreType.DMA(())],
        name=name, compiler_params=_cp(),
    )(x)


def _exchange(name, p3):
    _, r, c = p3.shape

    def body(p_ref, out_ref, send_sems, recv_sems, local_sem):
        me = _my_pos()
        my_id = _lin(me)
        mine = pltpu.make_async_copy(p_ref.at[my_id], out_ref.at[my_id], local_sem)
        mine.start()
        sends = []
        for k in range(1, N_DEV):
            peer = _flip(me, k)
            sends.append(pltpu.make_async_remote_copy(
                src_ref=p_ref.at[_lin(peer)], dst_ref=out_ref.at[my_id],
                send_sem=send_sems.at[k - 1], recv_sem=recv_sems.at[k - 1], device_id=peer, device_id_type=MESH))
        for cp in sends:
            cp.start()
        for k in range(1, N_DEV):
            peer = _flip(me, k)
            pltpu.make_async_remote_copy(
                src_ref=p_ref.at[my_id], dst_ref=out_ref.at[_lin(peer)],
                send_sem=send_sems.at[k - 1], recv_sem=recv_sems.at[k - 1], device_id=peer, device_id_type=MESH).wait_recv()
        for cp in sends:
            cp.wait_send()
        mine.wait()

    return pl.pallas_call(
        body, out_shape=SDS(p3.shape, p3.dtype),
        in_specs=[pl.BlockSpec(memory_space=pl.ANY)], out_specs=pl.BlockSpec(memory_space=pl.ANY),
        scratch_shapes=[pltpu.SemaphoreType.DMA((7,)), pltpu.SemaphoreType.DMA((7,)), pltpu.SemaphoreType.DMA(())],
        name=name, compiler_params=_cp(),
    )(p3)


def _sum_blocks(name, g3):
    n, r, c = g3.shape
    tr = _pick(r, (512, 256, 128, 64, 32, 16, 8))
    tc = _pick(c, (2048, 1024, 512, 256, 128))

    def body(g_ref, o_ref):
        acc = g_ref[0].astype(F32)
        for d in range(1, n):
            acc = acc + g_ref[d].astype(F32)
        o_ref[...] = acc

    return pl.pallas_call(
        body, grid=(r // tr, c // tc),
        in_specs=[pl.BlockSpec((n, tr, tc), lambda i, j: (0, i, j))],
        out_specs=pl.BlockSpec((tr, tc), lambda i, j: (i, j)),
        out_shape=SDS((r, c), F32), name=name, compiler_params=_cp(("parallel", "parallel")),
    )(g3)


def _reduce_scatter(name, p3):
    return _sum_blocks(name + "_sum", _exchange(name + "_xchg", p3))


def _all_reduce(name, v):
    return _sum_blocks(name + "_sum", _all_gather(name + "_ag", v))


def _conv_geometry(K, lc):
    tb = _pick(lc, (256, 128, 64, 32, 16, 8))
    hal = -(-(K // 2) // SUBLANES) * SUBLANES
    return tb, hal


def _conv_taps(i, K, tb, hal, lc, seg_c, seg_l, scr):
    is_ctx = i * tb < lc
    base = jnp.where(is_ctx, 0, lc)
    seg = jnp.where(is_ctx, seg_c, seg_l)
    pmod = (i * tb - base + lax.broadcasted_iota(jnp.int32, (tb, 1), 0)) & (seg - 1)
    for k in range(K):
        o = k - K // 2
        valid = (pmod + o >= 0) & (pmod + o < seg)
        yield k, scr[pl.ds(hal + o, tb), :], valid


def _conv_specs(tb, hal, cb, nrb, col0):
    q = tb // hal
    nh = nrb * q
    return [pl.BlockSpec((hal, cb), lambda j, i: (jnp.maximum(i * q - 1, 0), col0 + j)),
            pl.BlockSpec((tb, cb), lambda j, i: (i, col0 + j)),
            pl.BlockSpec((hal, cb), lambda j, i: (jnp.minimum((i + 1) * q, nh - 1), col0 + j))]


def _conv_fill(scr, prev_ref, cur_ref, next_ref, tb, hal):
    scr[pl.ds(0, hal), :] = prev_ref[...].astype(F32)
    scr[pl.ds(hal, tb), :] = cur_ref[...].astype(F32)
    scr[pl.ds(hal + tb, hal), :] = next_ref[...].astype(F32)


def _conv_fwd(name, x, xcol0, C, w, b, lc, seg_c, seg_l, out=None, ocol0=0):
    R = x.shape[0]
    K = w.shape[0]
    tb, hal = _conv_geometry(K, lc)
    cb = _pick(C, (512, 256, 128))
    nrb = R // tb
    wp = jnp.zeros((-(-K // SUBLANES) * SUBLANES, C), F32).at[:K].set(w)

    def body(*refs):
        prev_ref, cur_ref, next_ref, w_ref, b_ref = refs[:5]
        o_ref, scr = refs[-2], refs[-1]
        i = pl.program_id(1)
        _conv_fill(scr, prev_ref, cur_ref, next_ref, tb, hal)
        acc = jnp.broadcast_to(b_ref[...], (tb, cb))
        for k, tile, valid in _conv_taps(i, K, tb, hal, lc, seg_c, seg_l, scr):
            acc = acc + jnp.where(valid, tile, 0.0) * w_ref[pl.ds(k, 1), :]
        o_ref[...] = acc

    in_specs = _conv_specs(tb, hal, cb, nrb, xcol0 // cb)
    in_specs += [pl.BlockSpec((wp.shape[0], cb), lambda j, i: (0, j)), pl.BlockSpec((1, cb), lambda j, i: (0, j))]
    args = [x, x, x, wp, b.reshape(1, C)]
    aliases = {}
    if out is not None:
        in_specs.append(pl.BlockSpec(memory_space=pl.ANY))
        args.append(out)
        aliases = {5: 0}
    oc0 = ocol0 // cb
    return pl.pallas_call(
        body, grid=(C // cb, nrb), in_specs=in_specs,
        out_specs=pl.BlockSpec((tb, cb), lambda j, i: (i, oc0 + j)),
        out_shape=SDS((R, out.shape[1] if out is not None else C), F32),
        scratch_shapes=[pltpu.VMEM((tb + 2 * hal, cb), F32)], input_output_aliases=aliases,
        name=name, compiler_params=_cp(("parallel", "arbitrary")),
    )(*args)


def _conv_bwd_w(name, x, xcol0, dy, w_shape, lc, seg_c, seg_l):
    R = x.shape[0]
    K, C = w_shape
    tb, hal = _conv_geometry(K, lc)
    cb = _pick(C, (512, 256, 128))
    nrb = R // tb
    kp = -(-(K + 1) // SUBLANES) * SUBLANES

    def body(prev_ref, cur_ref, next_ref, dy_ref, o_ref, scr):
        i = pl.program_id(1)

        @pl.when(i == 0)
        def _():
            o_ref[...] = jnp.zeros_like(o_ref)

        _conv_fill(scr, prev_ref, cur_ref, next_ref, tb, hal)
        dy_t = dy_ref[...]
        for k, tile, valid in _conv_taps(i, K, tb, hal, lc, seg_c, seg_l, scr):
            o_ref[pl.ds(k, 1), :] += jnp.sum(jnp.where(valid, tile, 0.0) * dy_t, axis=0, keepdims=True)
        o_ref[pl.ds(K, 1), :] += jnp.sum(dy_t, axis=0, keepdims=True)

    in_specs = _conv_specs(tb, hal, cb, nrb, xcol0 // cb)
    in_specs.append(pl.BlockSpec((tb, cb), lambda j, i: (i, j)))
    res = pl.pallas_call(
        body, grid=(C // cb, nrb), in_specs=in_specs,
        out_specs=pl.BlockSpec((kp, cb), lambda j, i: (0, j)),
        out_shape=SDS((kp, C), F32), scratch_shapes=[pltpu.VMEM((tb + 2 * hal, cb), F32)],
        name=name, compiler_params=_cp(("parallel", "arbitrary")),
    )(x, x, x, dy)
    return res[:K], res[K]


def _bdot(dims):
    (ca, cb) = dims

    def raw(a, b, dn):
        return lax.dot_general(a.astype(BF16), b.astype(BF16), (dn, ((), ())), preferred_element_type=F32)

    @jax.custom_vjp
    def f(a, b):
        return raw(a, b, ((ca,), (cb,)))

    def fwd(a, b):
        return f(a, b), (a, b)

    def bwd(res, g):
        a, b = res
        if ca == 1:
            da = raw(g, b, ((1,), (1 - cb,)))
        else:
            da = raw(b, g, ((1 - cb,), (1,)))
        if cb == 0:
            db = raw(a, g, ((1 - ca,), (0,)))
        else:
            db = raw(g, a, ((0,), (1 - ca,)))
        return da, db

    f.defvjp(fwd, bwd)
    return f


_dot_nn = _bdot((1, 0))
_dot_nt = _bdot((1, 1))
_dot_tn = _bdot((0, 0))


def _split3(x):
    hi = x.astype(BF16)
    r1 = x - hi.astype(F32)
    mid = r1.astype(BF16)
    lo = (r1 - mid.astype(F32)).astype(BF16)
    return hi, mid, lo


def _tri_raw(tri, x):
    acc = None
    for part in _split3(x):
        t = jnp.dot(tri, part, preferred_element_type=F32)
        acc = t if acc is None else acc + t
    return acc


@jax.custom_vjp
def _tri_mm(tri, x):
    return _tri_raw(tri, x)


def _tri_mm_fwd(tri, x):
    return _tri_raw(tri, x), tri


def _tri_mm_bwd(tri, g):
    return jnp.zeros_like(tri), _tri_raw(tri.T, g)


_tri_mm.defvjp(_tri_mm_fwd, _tri_mm_bwd)


def _softplus(x):
    return jnp.maximum(x, 0.0) + jnp.log(1.0 + jnp.exp(-jnp.abs(x)))


def _dt_chunk(raw, bias, alog, nheads):
    q = raw.shape[0]
    dt = _softplus(raw + bias)
    dta = dt * (-jnp.exp(alog))
    li = lax.broadcasted_iota(jnp.int32, (q, q), 0)
    si = lax.broadcasted_iota(jnp.int32, (q, q), 1)
    lower = (si <= li).astype(BF16)
    upper = (si >= li).astype(BF16)
    col = lax.broadcasted_iota(jnp.int32, raw.shape, 1)
    cum = jnp.where(col < nheads, _tri_mm(lower, dta), _tri_mm(upper, dta))
    return dt.T, cum.T


def _dt_prep(name, proj, col_block, bias, alog, nheads):
    R = proj.shape[0]
    q = CHUNK

    def body(p_ref, b_ref, a_ref, dt_ref, cum_ref):
        dtt, cumt = _dt_chunk(p_ref[...], b_ref[...], a_ref[...], nheads)
        dt_ref[...] = dtt
        cum_ref[...] = cumt

    return pl.pallas_call(
        body, grid=(R // q,),
        in_specs=[pl.BlockSpec((q, LANES), lambda i: (i, col_block)),
                  pl.BlockSpec((1, LANES), lambda i: (0, 0)), pl.BlockSpec((1, LANES), lambda i: (0, 0))],
        out_specs=[pl.BlockSpec((LANES, q), lambda i: (0, i))] * 2,
        out_shape=[SDS((LANES, R), F32)] * 2, name=name, compiler_params=_cp(("parallel",)),
    )(proj, bias, alog)


def _dt_prep_bwd(name, proj, col_block, bias, alog, nheads, ddt, dcum, out, out_col_block):
    R = proj.shape[0]
    q = CHUNK
    wide = 4 * LANES

    def body(p_ref, b_ref, a_ref, g1_ref, g2_ref, out_any, o_ref, db_ref, da_ref):
        i = pl.program_id(0)
        _, vjp = jax.vjp(lambda r, b, a: _dt_chunk(r, b, a, nheads), p_ref[...], b_ref[...], a_ref[...])
        draw, db, da = vjp((g1_ref[...], g2_ref[...]))
        o_ref[...] = jnp.concatenate([draw, jnp.zeros((q, wide - LANES), F32)], axis=1)

        @pl.when(i == 0)
        def _():
            db_ref[...] = jnp.zeros_like(db_ref)
            da_ref[...] = jnp.zeros_like(da_ref)

        db_ref[...] += db
        da_ref[...] += da

    return pl.pallas_call(
        body, grid=(R // q,),
        in_specs=[pl.BlockSpec((q, LANES), lambda i: (i, col_block)),
                  pl.BlockSpec((1, LANES), lambda i: (0, 0)), pl.BlockSpec((1, LANES), lambda i: (0, 0)),
                  pl.BlockSpec((LANES, q), lambda i: (0, i)), pl.BlockSpec((LANES, q), lambda i: (0, i)),
                  pl.BlockSpec(memory_space=pl.ANY)],
        out_specs=[pl.BlockSpec((q, wide), lambda i: (i, out_col_block)),
                   pl.BlockSpec((1, LANES), lambda i: (0, 0)), pl.BlockSpec((1, LANES), lambda i: (0, 0))],
        out_shape=[SDS(out.shape, F32), SDS((1, LANES), F32), SDS((1, LANES), F32)],
        input_output_aliases={5: 0}, name=name, compiler_params=_cp(("arbitrary",)),
    )(proj, bias, alog, ddt, dcum, out)


def _ssd_chunk(rev, xpre, bpre, cpre, dt_rows, a_rows, hin, dvec):
    q, gw = xpre.shape
    hg = len(dt_rows)
    x, bm, cm = _silu(xpre), _silu(bpre), _silu(cpre)
    li = lax.broadcasted_iota(jnp.int32, (q, q), 0)
    si = lax.broadcasted_iota(jnp.int32, (q, q), 1)
    mask = (li <= si) if rev else (li >= si)
    lane = lax.broadcasted_iota(jnp.int32, (q, LANES), 1)
    lane1 = lax.broadcasted_iota(jnp.int32, (1, LANES), 1)
    laneq = lax.broadcasted_iota(jnp.int32, (1, q), 1)
    end = 0 if rev else q - 1
    scores = _dot_nt(cm, bm)
    dt_cols = [jnp.broadcast_to(r, (LANES, q)).T for r in dt_rows]
    a_cols = [jnp.broadcast_to(r, (LANES, q)).T for r in a_rows]
    tot = [jnp.sum(jnp.where(laneq == end, r, 0.0), axis=1, keepdims=True) for r in a_rows]
    pairs = hg // 2

    def expand(cols, n):
        return jnp.concatenate(
            [jnp.where((lane if n == q else lane1) < HEAD_P, cols[2 * p], cols[2 * p + 1]) for p in range(pairs)], axis=1)

    dt_exp = expand(dt_cols, q)
    a_exp = expand(a_cols, q)
    tot_exp = expand([jnp.broadcast_to(t, (1, LANES)) for t in tot], 1)
    xt = x * dt_exp
    ys = []
    for p in range(pairs):
        xp = xt[:, p * LANES:(p + 1) * LANES]
        acc = None
        for t in range(2):
            j = 2 * p + t
            seg = a_cols[j] - a_rows[j]
            m = scores * jnp.exp(jnp.where(mask, seg, -1e30))
            xm = jnp.where((lane < HEAD_P) if t == 0 else (lane >= HEAD_P), xp, 0.0)
            part = _dot_nn(m, xm)
            acc = part if acc is None else acc + part
        ys.append(acc)
    y_diag = jnp.concatenate(ys, axis=1)
    states = _dot_tn(bm, xt * jnp.exp(tot_exp - a_exp))
    hout = jnp.exp(tot_exp) * hin + states
    y_off = _dot_nn(cm, hin) * jnp.exp(a_exp)
    return y_diag + y_off + dvec * x, hout


def _chunk_order(rev, ncc, nct):
    if not rev:
        return lambda t: t
    return lambda t: jnp.where(t < ncc, ncc - 1 - t, nct - 1 - (t - ncc))


def _ssd_fwd(name, rev, xs, bs, cs, dtt, cumt, dvec, nheads, lc):
    R, DI = xs.shape
    q, n = CHUNK, N_STATE
    G = N_GROUPS
    gw = DI // G
    hg = gw // HEAD_P
    nct, ncc = R // q, lc // q
    order = _chunk_order(rev, ncc, nct)

    def body(x_ref, b_ref, c_ref, dt_ref, cum_ref, d_ref, y_ref, st_ref, h_scr):
        g, t = pl.program_id(0), pl.program_id(1)

        @pl.when(t == 0)
        def _():
            h_scr[...] = jnp.zeros_like(h_scr)

        base = (nheads if rev else 0) + g * hg
        dt_rows = [dt_ref[pl.ds(base + j, 1), :] for j in range(hg)]
        a_rows = [cum_ref[pl.ds(base + j, 1), :] for j in range(hg)]
        hin = h_scr[...]
        st_ref[...] = hin
        y, hout = _ssd_chunk(rev, x_ref[...], b_ref[...], c_ref[...], dt_rows, a_rows, hin, d_ref[...])
        y_ref[...] = y
        h_scr[...] = hout

    return pl.pallas_call(
        body, grid=(G, nct),
        in_specs=[pl.BlockSpec((q, gw), lambda g, t: (order(t), g)),
                  pl.BlockSpec((q, n), lambda g, t: (order(t), g)),
                  pl.BlockSpec((q, n), lambda g, t: (order(t), g)),
                  pl.BlockSpec((LANES, q), lambda g, t: (0, order(t))),
                  pl.BlockSpec((LANES, q), lambda g, t: (0, order(t))),
                  pl.BlockSpec((1, gw), lambda g, t: (0, g))],
        out_specs=[pl.BlockSpec((q, gw), lambda g, t: (order(t), g)),
                   pl.BlockSpec((None, None, n, gw), lambda g, t: (order(t), g, 0, 0))],
        out_shape=[SDS((R, DI), F32), SDS((nct, G, n, gw), F32)],
        scratch_shapes=[pltpu.VMEM((n, gw), F32)],
        name=name, compiler_params=_cp(("parallel", "arbitrary")),
    )(xs, bs, cs, dtt, cumt, dvec)


def _ssd_bwd(name, rev, xs, bs, cs, dtt, cumt, dvec, st, dy, nheads, lc, acc=None):
    R, DI = xs.shape
    q, n = CHUNK, N_STATE
    G = N_GROUPS
    gw = DI // G
    hg = gw // HEAD_P
    hgp = -(-hg // SUBLANES) * SUBLANES
    nct, ncc = R // q, lc // q
    fwd_order = _chunk_order(rev, ncc, nct)
    order = lambda t: fwd_order(nct - 1 - t)
    has_acc = acc is not None

    def body(*refs):
        x_ref, b_ref, c_ref, dt_ref, cum_ref, d_ref, st_ref, dy_ref = refs[:8]
        k = 8
        acc_refs = refs[k:k + 3] if has_acc else None
        k += 3 if has_acc else 0
        dx_ref, db_ref, dc_ref, ddt_ref, dcum_ref, dd_ref, dh_scr = refs[k:]
        g, t = pl.program_id(0), pl.program_id(1)

        @pl.when(t == 0)
        def _():
            dh_scr[...] = jnp.zeros_like(dh_scr)
            dd_ref[...] = jnp.zeros_like(dd_ref)

        base = (nheads if rev else 0) + g * hg
        dt_rows = [dt_ref[pl.ds(base + j, 1), :] for j in range(hg)]
        a_rows = [cum_ref[pl.ds(base + j, 1), :] for j in range(hg)]
        _, vjp = jax.vjp(functools.partial(_ssd_chunk, rev), x_ref[...], b_ref[...], c_ref[...],
                         dt_rows, a_rows, st_ref[...], d_ref[...])
        dx, db, dc, ddt, da, dhin, dd = vjp((dy_ref[...], dh_scr[...]))
        if has_acc:
            dx, db, dc = dx + acc_refs[0][...], db + acc_refs[1][...], dc + acc_refs[2][...]
        dx_ref[...] = dx
        db_ref[...] = db
        dc_ref[...] = dc
        if hgp > hg:
            ddt_ref[...] = jnp.zeros_like(ddt_ref)
            dcum_ref[...] = jnp.zeros_like(dcum_ref)
        for j in range(hg):
            ddt_ref[pl.ds(j, 1), :] = ddt[j]
            dcum_ref[pl.ds(j, 1), :] = da[j]
        dd_ref[...] += dd
        dh_scr[...] = dhin

    row_specs = [pl.BlockSpec((q, gw), lambda g, t: (order(t), g)),
                 pl.BlockSpec((q, n), lambda g, t: (order(t), g)),
                 pl.BlockSpec((q, n), lambda g, t: (order(t), g))]
    in_specs = row_specs + [
        pl.BlockSpec((LANES, q), lambda g, t: (0, order(t))),
        pl.BlockSpec((LANES, q), lambda g, t: (0, order(t))),
        pl.BlockSpec((1, gw), lambda g, t: (0, g)),
        pl.BlockSpec((None, None, n, gw), lambda g, t: (order(t), g, 0, 0)),
        pl.BlockSpec((q, gw), lambda g, t: (order(t), g))]
    args = [xs, bs, cs, dtt, cumt, dvec, st, dy]
    aliases = {}
    if has_acc:
        in_specs += row_specs
        args += list(acc)
        aliases = {8: 0, 9: 1, 10: 2}
    return pl.pallas_call(
        body, grid=(G, nct), in_specs=in_specs,
        out_specs=row_specs + [pl.BlockSpec((None, hgp, q), lambda g, t: (g, 0, order(t))),
                               pl.BlockSpec((None, hgp, q), lambda g, t: (g, 0, order(t))),
                               pl.BlockSpec((1, gw), lambda g, t: (0, g))],
        out_shape=[SDS(xs.shape, F32), SDS(bs.shape, F32), SDS(cs.shape, F32),
                   SDS((G, hgp, R), F32), SDS((G, hgp, R), F32), SDS((1, DI), F32)],
        scratch_shapes=[pltpu.VMEM((n, gw), F32)], input_output_aliases=aliases,
        name=name, compiler_params=_cp(("parallel", "arbitrary")),
    )(*args)


def _row_tile(width):
    return 256 if width <= 2048 else 128


def _rowwise_op(name, f, n_rows, out_widths):
    def fwd(*args):
        rows, vecs = args[:n_rows], args[n_rows:]
        R = rows[0].shape[0]
        tm = min(R, _row_tile(max([a.shape[1] for a in rows] + list(out_widths))))
        outs = _rowwise(name, lambda r0, rt, vt: (f(r0, rt, vt), []), R, tm,
                        [(a, a.shape[1], 0) for a in rows], list(vecs),
                        [(w, F32, None, 0, None) for w in out_widths])
        return (outs[0] if len(outs) == 1 else tuple(outs)), args

    def bwd(args, ct):
        rows, vecs = args[:n_rows], args[n_rows:]
        cts = [ct] if len(out_widths) == 1 else list(ct)
        R = rows[0].shape[0]
        tm = min(R, _row_tile(max([a.shape[1] for a in rows] + list(out_widths))))
        fb = _vjp_fn(f, n_rows, [True] * n_rows, [True] * len(vecs))
        res = _rowwise(name + "_bwd", fb, R, tm, [(a, a.shape[1], 0) for a in list(rows) + cts], list(vecs),
                       [(a.shape[1], F32, None, 0, None) for a in rows], [v.shape for v in vecs])
        return tuple(res)

    return _cvjp(fwd, bwd)


def _norm_mod_op(name, lc):
    def f(row0, rows, vecs):
        (h,), (g, shc, scc, shl, scl) = rows, vecs
        ctx = _is_ctx(row0, h.shape[0], lc)
        return [_rms(h, g) * (1.0 + jnp.where(ctx, scc, scl)) + jnp.where(ctx, shc, shl)]

    def fwd(h, *vecs):
        R, D = h.shape
        (u,) = _rowwise(name, lambda r0, rt, vt: (f(r0, rt, vt), []), R, min(R, _row_tile(D)),
                        [(h, D, 0)], list(vecs), [(D, F32, None, 0, None)])
        return (u, h), (h,) + vecs

    def bwd(res, ct):
        h, vecs = res[0], res[1:]
        du, dhp = ct
        R, D = h.shape
        fb = _vjp_fn(f, 1, [True], [True] * 5, add_last=True)
        out = _rowwise(name + "_bwd", fb, R, min(R, _row_tile(D)), [(h, D, 0), (du, D, 0), (dhp, D, 0)], list(vecs),
                       [(D, F32, None, 0, None)], [v.shape for v in vecs])
        return tuple(out)

    return _cvjp(fwd, bwd)


def _post_res_op(name, lc):
    def branch(row0, rows, vecs):
        (y,), (g, gc, gl, bias) = rows, vecs
        return [jnp.where(_is_ctx(row0, y.shape[0], lc), gc, gl) * _rms(y + bias, g)]

    def full(row0, rows, vecs):
        return [rows[0] + branch(row0, rows[1:], vecs)[0]]

    def fwd(h, y, *vecs):
        R, D = h.shape
        (out,) = _rowwise(name, lambda r0, rt, vt: (full(r0, rt, vt), []), R, min(R, _row_tile(D)),
                          [(h, D, 0), (y, D, 0)], list(vecs), [(D, F32, None, 0, None)])
        return out, (y,) + vecs

    def bwd(res, d):
        y, vecs = res[0], res[1:]
        R, D = y.shape
        fb = _vjp_fn(branch, 1, [True], [True] * 4)
        out = _rowwise(name + "_bwd", fb, R, min(R, _row_tile(D)), [(y, D, 0), (d, D, 0)], list(vecs),
                       [(D, F32, None, 0, None)], [v.shape for v in vecs])
        return (d,) + tuple(out)

    return _cvjp(fwd, bwd)


def _gather_weight(name, w, layout, npad=0):
    wb = w.astype(BF16)
    if layout == "col":
        return _all_gather(name, wb)
    if layout == "row":
        g = _all_gather(name, wb)
        return g.reshape(1, N_DEV * w.shape[0], w.shape[1])
    k, ns = w.shape
    g = _all_gather(name, wb.reshape(k * ns // LANES, LANES)).reshape(N_DEV, k, ns)
    full = jnp.transpose(g, (1, 0, 2)).reshape(k, N_DEV * ns)
    return jnp.pad(full, ((0, 0), (0, npad - N_DEV * ns)))[None]


def _scatter_grad(name, dw3, w_shape, layout):
    if layout == "col":
        return _reduce_scatter(name, dw3)
    if layout == "row":
        return _reduce_scatter(name, dw3.reshape(N_DEV, w_shape[0], w_shape[1]))
    k, ns = w_shape
    p = dw3[0, :, :N_DEV * ns].reshape(k, N_DEV, ns)
    p = jnp.transpose(p, (1, 0, 2)).reshape(N_DEV, k * ns // LANES, LANES)
    return _reduce_scatter(name, p).reshape(k, ns)


def _mm_op(name, layout, w_shape, npad=0):
    def fwd(a, w):
        b3 = _gather_weight(name + "_ag", w, layout, npad)
        return _mm_nn(name, a, b3), (a, b3)

    def bwd(res, dc):
        a, b3 = res
        da = _mm_nt(name + "_da", dc, b3)
        dw3 = _mm_tn(name + "_dw", a, dc, N_DEV if layout == "col" else 1)
        return da, _scatter_grad(name + "_rs", dw3, w_shape, layout)

    return _cvjp(fwd, bwd)


def _mlp_op(name):
    def fwd(u, w1, w2):
        b1 = _gather_weight(name + "_ag1", w1, "col")
        b2 = _gather_weight(name + "_ag2", w2, "row")
        act = _mm_nn(name + "_up", u, b1, out_dtype=BF16, epilogue=lambda acc: jnp.square(jnp.maximum(acc, 0.0)))
        return _mm_nn(name + "_down", act, b2), (u, act, b1, b2)

    def bwd(res, df):
        u, act, b1, b2 = res
        da = _mm_nt(name + "_dact", df, b2, out_dtype=BF16, extra=act,
                    epilogue=lambda acc, a: acc * (2.0 * jnp.sqrt(a.astype(F32))))
        dw2 = _mm_tn(name + "_dw2", act, df, 1)
        dw1 = _mm_tn(name + "_dw1", u, da, N_DEV)
        du = _mm_nt(name + "_du", da, b1)
        ks, n2 = b2.shape[1] // N_DEV, b2.shape[2]
        return (du, _scatter_grad(name + "_rs1", dw1, None, "col"),
                _scatter_grad(name + "_rs2", dw2, (ks, n2), "row"))

    return _cvjp(fwd, bwd)


def _conv_op(name, lc, seg_c, seg_l):
    def fwd(v, w, b):
        return _conv_fwd(name, v, 0, v.shape[1], w, b, lc, seg_c, seg_l), (v, w)

    def bwd(res, dy):
        v, w = res
        C = v.shape[1]
        dv = _conv_fwd(name + "_dx", dy, 0, C, w[::-1], jnp.zeros((C,), F32), lc, seg_c, seg_l)
        dw, db = _conv_bwd_w(name + "_dw", v, 0, dy, w.shape, lc, seg_c, seg_l)
        return dv, dw, db

    return _cvjp(fwd, bwd)


def _ssm_core_op(name, cfg):
    DI, GN, H, lc, L, NP = cfg["DI"], cfg["GN"], cfg["H"], cfg["Lc"], cfg["L"], cfg["NP"]
    parts = [(DI, DI), (2 * DI, GN), (2 * DI + GN, GN)]
    dt_col = 2 * DI + 2 * GN

    def gated(row0, rows, vecs):
        (yf, yb, z), (g,) = rows, vecs
        return [_rms((yf + yb) * _silu(z), g)]

    def fwd(proj, cw, cb, alog, bias, d_f, d_b, ng):
        R = proj.shape[0]
        pre, off = [], 0
        for idx, (col, wd) in enumerate(parts):
            pre.append(_conv_fwd(f"{name}_conv{idx}", proj, col, wd, cw[:, off:off + wd], cb[off:off + wd], lc, lc, L))
            off += wd
        dtt, cumt = _dt_prep(name + "_dt", proj, dt_col // LANES, bias, alog, H)
        yf, stf = _ssd_fwd(name + "_scan_f", False, *pre, dtt, cumt, d_f, H, lc)
        yb, stb = _ssd_fwd(name + "_scan_b", True, *pre, dtt, cumt, d_b, H, lc)
        tm = min(R, _row_tile(DI))
        (out,) = _rowwise(name + "_gate", lambda r0, rt, vt: (gated(r0, rt, vt), []), R, tm,
                          [(yf, DI, 0), (yb, DI, 0), (proj, DI, 0)], [ng], [(DI, F32, None, 0, None)])
        return out, (proj, cw, alog, bias, d_f, d_b, ng, pre, dtt, cumt, yf, yb, stf, stb)

    def bwd(res, dout):
        proj, cw, alog, bias, d_f, d_b, ng, pre, dtt, cumt, yf, yb, stf, stb = res
        R = proj.shape[0]
        tm = min(R, _row_tile(DI))
        fb = _vjp_fn(gated, 3, [True, False, True], [True])
        dy, dproj, dng = _rowwise(name + "_gate_bwd", fb, R, tm,
                                  [(yf, DI, 0), (yb, DI, 0), (proj, DI, 0), (dout, DI, 0)], [ng],
                                  [(DI, F32, None, 0, None), (DI, F32, NP, 0, None)], [ng.shape])
        rf = _ssd_bwd(name + "_scan_f_bwd", False, *pre, dtt, cumt, d_f, stf, dy, H, lc)
        rb = _ssd_bwd(name + "_scan_b_bwd", True, *pre, dtt, cumt, d_b, stb, dy, H, lc, acc=rf[:3])
        hg = DI // N_GROUPS // HEAD_P

        def head_rows(t3):
            return t3[:, :hg].reshape(H, R)

        zpad = jnp.zeros((LANES - 2 * H, R), F32)
        ddt = jnp.concatenate([head_rows(rf[3]), head_rows(rb[3]), zpad], axis=0)
        dcum = jnp.concatenate([head_rows(rf[4]), head_rows(rb[4]), zpad], axis=0)
        dproj, dbias, dalog = _dt_prep_bwd(name + "_dt_bwd", proj, dt_col // LANES, bias, alog, H, ddt, dcum,
                                           dproj, dt_col // (4 * LANES))
        dws, dbs, off = [], [], 0
        for idx, (col, wd) in enumerate(parts):
            dpart = rb[idx]
            dproj = _conv_fwd(f"{name}_conv{idx}_dx", dpart, 0, wd, cw[::-1, off:off + wd], jnp.zeros((wd,), F32),
                              lc, lc, L, out=dproj, ocol0=col)
            dw, db = _conv_bwd_w(f"{name}_conv{idx}_dw", proj, col, dpart, (cw.shape[0], wd), lc, lc, L)
            dws.append(dw)
            dbs.append(db)
            off += wd
        return (dproj, jnp.concatenate(dws, axis=1), jnp.concatenate(dbs), dalog, dbias, rf[5], rb[5], dng)

    return _cvjp(fwd, bwd)


def _mod_op(name):
    def fwd(s8, w, b):
        depth = w.shape[0]
        b3s = [_all_gather(f"{name}_ag{i}", w[i].astype(BF16)) for i in range(depth)]
        m = [_mm_nn(f"{name}_mm{i}", s8, b3s[i], bias=b[i][None]) for i in range(depth)]
        return jnp.stack(m), (s8, tuple(b3s))

    def bwd(res, dm):
        s8, b3s = res
        depth = len(b3s)
        d = s8.shape[1]
        ns = b3s[0].shape[2]
        me = _lin(_my_pos())
        ds = sum(_mm_nt(f"{name}_ds{i}", dm[i], b3s[i]) for i in range(depth))
        fac = jnp.concatenate([s8] + [dm[i] for i in range(depth)], axis=1)
        allf = _all_gather(name + "_fac", fac).reshape(N_DEV * SUBLANES, fac.shape[1])
        dws = []
        for i in range(depth):
            dm_i = lax.dynamic_slice_in_dim(allf, d + i * N_DEV * ns + me * ns, ns, axis=1)
            dws.append(_mm_tn(f"{name}_dw{i}", allf[:, :d], dm_i, 1, out_dtype=F32)[0])
        return ds, jnp.stack(dws), jnp.sum(dm, axis=1)

    return _cvjp(fwd, bwd)


def _loss_grad(name, h, target, lc):
    R, D = h.shape
    tm = min(lc, 256)
    off = lc // tm

    def body(h_ref, t_ref, g_ref, s_ref):
        i = pl.program_id(0)

        @pl.when(i == 0)
        def _():
            s_ref[...] = jnp.zeros_like(s_ref)

        err = jnp.where(i >= off, h_ref[...] - t_ref[...], 0.0)
        g_ref[...] = err * (1.0 / D)
        s_ref[...] += jnp.sum(err * err, axis=0, keepdims=True)

    return pl.pallas_call(
        body, grid=(R // tm,),
        in_specs=[pl.BlockSpec((tm, D), lambda i: (i, 0)),
                  pl.BlockSpec((tm, D), lambda i: (jnp.maximum(i - off, 0), 0))],
        out_specs=[pl.BlockSpec((tm, D), lambda i: (i, 0)), pl.BlockSpec((1, D), lambda i: (0, 0))],
        out_shape=[SDS((R, D), F32), SDS((1, D), F32)], name=name, compiler_params=_cp(("arbitrary",)),
    )(h, target)


def _adam(name, w, g, m, v):
    shape = w.shape
    c = shape[-1]
    r = w.size // c
    tr = _pick(r, (256, 128, 64, 32, 16, 8))
    tc = _pick(c, (2048, 1536, 1024, 768, 512, 256, 128))
    c1 = 1.0 / (1.0 - ADAM_B1 ** ADAM_STEP)
    c2 = 1.0 / (1.0 - ADAM_B2 ** ADAM_STEP)

    def body(w_ref, g_ref, m_ref, v_ref, d_ref, mo_ref, vo_ref):
        gg = g_ref[...]
        mn = ADAM_B1 * m_ref[...] + (1.0 - ADAM_B1) * gg
        vn = ADAM_B2 * v_ref[...] + (1.0 - ADAM_B2) * jnp.square(gg)
        d_ref[...] = -ADAM_LR * ((mn * c1) / (jnp.sqrt(vn * c2) + ADAM_EPS) + ADAM_WD * w_ref[...])
        mo_ref[...] = mn
        vo_ref[...] = vn

    spec = pl.BlockSpec((tr, tc), lambda i, j: (i, j))
    res = pl.pallas_call(
        body, grid=(r // tr, c // tc), in_specs=[spec] * 4, out_specs=[spec] * 3,
        out_shape=[SDS((r, c), F32)] * 3, name=name, compiler_params=_cp(("parallel", "parallel")),
    )(*[a.reshape(r, c) for a in (w, g, m, v)])
    return [a.reshape(shape) for a in res]


WEIGHTS = ["c_ctx", "mod_w", "mod_b", "pre_mix_g", "post_mix_g", "pre_mlp_g", "post_mlp_g", "mlp_w1", "mlp_w2",
           "ssm_in_w", "ssm_conv_w", "ssm_conv_b", "ssm_a_log_f", "ssm_dt_bias_f", "ssm_d_f", "ssm_a_log_b",
           "ssm_dt_bias_b", "ssm_d_b", "ssm_norm_g", "ssm_out_w", "conf_pw1_w", "conf_pw1_b", "conf_dw_w",
           "conf_dw_b", "conf_ln_g", "conf_ln_b", "conf_pw2_w", "conf_pw2_b"]
BIG_SHARDED = ["mod_w", "mlp_w1", "mlp_w2", "ssm_in_w", "ssm_out_w", "conf_pw1_w", "conf_pw2_w"]
SMALL_SHARDED = {"ssm_conv_w": 2, "conf_pw1_b": 1, "conf_dw_w": 2, "conf_dw_b": 1, "conf_ln_g": 1, "conf_ln_b": 1,
                 "conf_pw2_b": 1}


def _pack(arrs):
    flat = jnp.concatenate([a.reshape(-1).astype(F32) for a in arrs])
    n = flat.shape[0]
    tile = LANES * SUBLANES
    npad = -(-n // tile) * tile
    return jnp.pad(flat, (0, npad - n)).reshape(npad // LANES, LANES)


def _unpack(flat, shapes):
    out, off = [], 0
    for s in shapes:
        n = 1
        for d in s:
            n *= d
        out.append(flat[off:off + n].reshape(s))
        off += n
    return out


def _to_col(h, lc, rows_g):
    d = h.shape[1]
    lat = h[lc:].reshape(rows_g, GRID_W, d).swapaxes(0, 1).reshape(-1, d)
    return jnp.concatenate([h[:lc], lat], axis=0)


def _from_col(h, lc, rows_g):
    d = h.shape[1]
    lat = h[lc:].reshape(GRID_W, rows_g, d).swapaxes(0, 1).reshape(-1, d)
    return jnp.concatenate([h[:lc], lat], axis=0)


def _step(a):
    x, ctx, target = a["x"][0], a["ctx"][0], a["loss_target"][0]
    L, D = x.shape
    lc = ctx.shape[0]
    depth = a["mod_w"].shape[0]
    DI = 2 * D
    H = DI // HEAD_P
    GN = N_GROUPS * N_STATE
    cfg = dict(DI=DI, GN=GN, H=H, Lc=lc, L=L, NP=2 * DI + 2 * GN + 4 * LANES)
    rows_g = L // GRID_W
    me = _lin(_my_pos())

    names = list(SMALL_SHARDED)
    gathered = _all_gather("small_w_ag", _pack([a[n] for n in names])).reshape(N_DEV, -1)
    full_small, off = {}, 0
    for n in names:
        shp, ax = a[n].shape, SMALL_SHARDED[n]
        seg = gathered[:, off:off + a[n].size].reshape((N_DEV,) + shp)
        off += a[n].size
        full_small[n] = jnp.moveaxis(seg, 0, ax).reshape(shp[:ax] + (N_DEV * shp[ax],) + shp[ax + 1:])

    params = {n: (full_small[n] if n in SMALL_SHARDED else a[n]) for n in WEIGHTS}
    params["x"] = x

    def trunk(p):
        raw8 = jnp.concatenate([a["c"], p["c_ctx"][None], jnp.zeros((SUBLANES - 2, D), F32)], axis=0)
        s8 = _rowwise_op("silu_c", lambda r0, rows, vecs: [_silu(rows[0])], 1, [D])(raw8)
        mods = _mod_op("mod")(s8, p["mod_w"], p["mod_b"])
        zero_bias = jnp.zeros((1, D), F32)
        h = jnp.concatenate([ctx, p["x"]], axis=0)
        for i in range(depth):
            kind, j = i % 2, i // 2
            col_major = (j % 2) == 1
            if i == 2:
                h = _to_col(h, lc, rows_g)
            sh1, sc1, g1, sh2, sc2, g2 = jnp.split(mods[i], 6, axis=1)
            u, hp = _norm_mod_op(f"l{i}_pre_mix", lc)(h, p["pre_mix_g"][i][None], sh1[1:2], sc1[1:2], sh1[0:1], sc1[0:1])
            if kind == 0:
                w_in = p["ssm_in_w"][j]
                proj = _mm_op(f"l{i}_in", "flat", w_in.shape, cfg["NP"])(u, w_in)
                pad = jnp.zeros((LANES - 2 * H,), F32)
                alog = jnp.concatenate([p["ssm_a_log_f"][j], p["ssm_a_log_b"][j], pad])[None]
                bias = jnp.concatenate([p["ssm_dt_bias_f"][j], p["ssm_dt_bias_b"][j], pad])[None]
                y = _ssm_core_op(f"l{i}_ssm", cfg)(
                    proj, p["ssm_conv_w"][j], p["ssm_conv_b"][j], alog, bias,
                    jnp.repeat(p["ssm_d_f"][j], HEAD_P)[None], jnp.repeat(p["ssm_d_b"][j], HEAD_P)[None],
                    p["ssm_norm_g"][j][None])
                w_out = p["ssm_out_w"][j]
                o = _mm_op(f"l{i}_out", "row", w_out.shape)(y, w_out)
                mix_bias = zero_bias
            else:
                w1 = p["conf_pw1_w"][j]
                pre = _mm_op(f"l{i}_pw1", "col", w1.shape)(u, w1)
                glu = _rowwise_op(f"l{i}_glu", lambda r0, rows, vecs: [
                    (rows[0] + vecs[0])[:, :D] * jax.nn.sigmoid((rows[0] + vecs[0])[:, D:])], 1, [D])
                v = glu(pre, p["conf_pw1_b"][j][None])
                seg = rows_g if col_major else GRID_W
                cv = _conv_op(f"l{i}_dw", lc, lc, seg)(v, p["conf_dw_w"][j], p["conf_dw_b"][j])

                def ln_swish(r0, rows, vecs):
                    xc = rows[0] - jnp.mean(rows[0], axis=-1, keepdims=True)
                    yn = xc * lax.rsqrt(jnp.mean(xc * xc, axis=-1, keepdims=True) + EPS) * vecs[0] + vecs[1]
                    return [_silu(yn)]

                w = _rowwise_op(f"l{i}_ln", ln_swish, 1, [D])(cv, p["conf_ln_g"][j][None], p["conf_ln_b"][j][None])
                w2 = p["conf_pw2_w"][j]
                o = _mm_op(f"l{i}_pw2", "row", w2.shape)(w, w2)
                mix_bias = p["conf_pw2_b"][j][None]
            h = _post_res_op(f"l{i}_post_mix", lc)(hp, o, p["post_mix_g"][i][None], g1[1:2], g1[0:1], mix_bias)
            u2, hp = _norm_mod_op(f"l{i}_pre_mlp", lc)(h, p["pre_mlp_g"][i][None], sh2[1:2], sc2[1:2], sh2[0:1], sc2[0:1])
            f = _mlp_op(f"l{i}_mlp")(u2, p["mlp_w1"][i], p["mlp_w2"][i])
            h = _post_res_op(f"l{i}_post_mlp", lc)(hp, f, p["post_mlp_g"][i][None], g2[1:2], g2[0:1], zero_bias)
        return _from_col(h, lc, rows_g) if depth > 2 else h

    h_out, vjp = jax.vjp(trunk, params)
    dh, sq = _loss_grad("loss", h_out, target, lc)
    (grads,) = vjp(dh)
    loss = lax.psum(jnp.sum(sq) * (0.5 / D), ("x", "y", "c"))

    part_names = [n for n in WEIGHTS if n not in BIG_SHARDED]
    red = _all_reduce("small_g", _pack([grads[n] for n in part_names])).reshape(-1)
    reduced = dict(zip(part_names, _unpack(red, [grads[n].shape for n in part_names])))
    final = {}
    for n in WEIGHTS:
        if n in BIG_SHARDED:
            final[n] = grads[n]
        elif n in SMALL_SHARDED:
            ax = SMALL_SHARDED[n]
            final[n] = lax.dynamic_slice_in_dim(reduced[n], me * a[n].shape[ax], a[n].shape[ax], axis=ax)
        else:
            final[n] = reduced[n]

    deltas, new_m, new_v = [], [], []
    for n in WEIGHTS:
        d_, m_, v_ = _adam("adam_" + n, a[n], final[n], a["m_" + n], a["v_" + n])
        deltas.append(d_)
        new_m.append(m_)
        new_v.append(v_)
    return (loss, grads["x"][None], *[final[n] for n in WEIGHTS], *deltas, *new_m, *new_v)


def kernel(x, c, ctx, c_ctx, mod_w, mod_b, pre_mix_g, post_mix_g, pre_mlp_g, post_mlp_g, mlp_w1, mlp_w2, ssm_in_w, ssm_conv_w, ssm_conv_b, ssm_a_log_f, ssm_dt_bias_f, ssm_d_f, ssm_a_log_b, ssm_dt_bias_b, ssm_d_b, ssm_norm_g, ssm_out_w, conf_pw1_w, conf_pw1_b, conf_dw_w, conf_dw_b, conf_ln_g, conf_ln_b, conf_pw2_w, conf_pw2_b, loss_target, m_c_ctx, m_mod_w, m_mod_b, m_pre_mix_g, m_post_mix_g, m_pre_mlp_g, m_post_mlp_g, m_mlp_w1, m_mlp_w2, m_ssm_in_w, m_ssm_conv_w, m_ssm_conv_b, m_ssm_a_log_f, m_ssm_dt_bias_f, m_ssm_d_f, m_ssm_a_log_b, m_ssm_dt_bias_b, m_ssm_d_b, m_ssm_norm_g, m_ssm_out_w, m_conf_pw1_w, m_conf_pw1_b, m_conf_dw_w, m_conf_dw_b, m_conf_ln_g, m_conf_ln_b, m_conf_pw2_w, m_conf_pw2_b, v_c_ctx, v_mod_w, v_mod_b, v_pre_mix_g, v_post_mix_g, v_pre_mlp_g, v_post_mlp_g, v_mlp_w1, v_mlp_w2, v_ssm_in_w, v_ssm_conv_w, v_ssm_conv_b, v_ssm_a_log_f, v_ssm_dt_bias_f, v_ssm_d_f, v_ssm_a_log_b, v_ssm_dt_bias_b, v_ssm_d_b, v_ssm_norm_g, v_ssm_out_w, v_conf_pw1_w, v_conf_pw1_b, v_conf_dw_w, v_conf_dw_b, v_conf_ln_g, v_conf_ln_b, v_conf_pw2_w, v_conf_pw2_b):
    return _step(dict(locals()))
```

```python
import functools

import jax
import jax.numpy as jnp
from jax import lax
from jax.experimental import pallas as pl
from jax.experimental.pallas import tpu as pltpu

F32, BF16 = jnp.float32, jnp.bfloat16
SDS = jax.ShapeDtypeStruct

GRID_W = 64
HEAD_P = 64
N_GROUPS = 8
N_STATE = 128
SSM_K = 5
CHUNK = 128
CONF_K = 31
EPS = 1e-6
ADAM_LR, ADAM_B1, ADAM_B2, ADAM_EPS, ADAM_WD, ADAM_STEP = 0.001, 0.9, 0.999, 1e-08, 0.01, 10

LANES = 128
SUBLANES = 8
N_DEV = 8
MESH = pl.DeviceIdType.MESH
VMEM_LIMIT = 56 * 2**20


def _cp(sem=None, vmem=VMEM_LIMIT):
    if sem is None:
        return pltpu.CompilerParams(vmem_limit_bytes=vmem)
    return pltpu.CompilerParams(dimension_semantics=sem, vmem_limit_bytes=vmem)


def _pick(dim, cands):
    for c in cands:
        if c <= dim and dim % c == 0:
            return c
    return dim


def _cvjp(fwd, bwd):
    @jax.custom_vjp
    def op(*args):
        return fwd(*args)[0]
    op.defvjp(fwd, bwd)
    return op


def _rowwise(name, fn, R, tm, rows, vecs, outs, vouts=()):
    n_r, n_v, n_o, n_vo = len(rows), len(vecs), len(outs), len(vouts)
    alias_in = [o[4] for o in outs if o[4] is not None]
    n_a = len(alias_in)

    def body(*refs):
        r = refs[:n_r]
        v = refs[n_r:n_r + n_v]
        o = refs[n_r + n_v + n_a:n_r + n_v + n_a + n_o]
        vo = refs[n_r + n_v + n_a + n_o:]
        i = pl.program_id(0)
        ro, vvals = fn(i * tm, [x[...] for x in r], [x[...] for x in v])
        for ref, val in zip(o, ro):
            ref[...] = val.astype(ref.dtype)
        if n_vo:
            @pl.when(i == 0)
            def _():
                for ref in vo:
                    ref[...] = jnp.zeros_like(ref)
            for ref, val in zip(vo, vvals):
                ref[...] += val

    def colmap(cb):
        return lambda i: (i, cb)

    in_specs = [pl.BlockSpec((tm, w), colmap(cb)) for (_, w, cb) in rows]
    in_specs += [pl.BlockSpec(a.shape, lambda i: (0, 0)) for a in vecs]
    in_specs += [pl.BlockSpec(memory_space=pl.ANY)] * n_a
    out_shape, out_specs, aliases = [], [], {}
    ai = 0
    for k, (w, dt, tot, cb, al) in enumerate(outs):
        out_shape.append(SDS((R, tot if tot else w), dt))
        out_specs.append(pl.BlockSpec((tm, w), colmap(cb)))
        if al is not None:
            aliases[n_r + n_v + ai] = k
            ai += 1
    for (vv, w) in vouts:
        out_shape.append(SDS((vv, w), F32))
        out_specs.append(pl.BlockSpec((vv, w), lambda i: (0, 0)))
    res = pl.pallas_call(
        body, grid=(R // tm,), in_specs=in_specs, out_specs=out_specs, out_shape=out_shape,
        input_output_aliases=aliases, name=name, compiler_params=_cp(("arbitrary",)),
    )(*[a for a, _, _ in rows], *vecs, *alias_in)
    return list(res)


def _vjp_fn(f, n_rows, want_rows, want_vecs, add_last=False):
    def fb(row0, tiles, vecs):
        rows = [t.astype(F32) for t in tiles[:n_rows]]
        outs, vjp = jax.vjp(lambda rs, vs: f(row0, rs, vs), rows, [v.astype(F32) for v in vecs])
        cts = [t.astype(F32) for t in tiles[n_rows:n_rows + len(outs)]]
        g_rows, g_vecs = vjp(cts)
        gr = [g for g, w in zip(g_rows, want_rows) if w]
        if add_last:
            gr[0] = gr[0] + tiles[-1].astype(F32)
        return gr, [g for g, w in zip(g_vecs, want_vecs) if w]
    return fb


def _silu(x):
    return x * jax.nn.sigmoid(x)


def _rms(x, g):
    return x * lax.rsqrt(jnp.mean(x * x, axis=-1, keepdims=True) + EPS) * g


def _is_ctx(row0, tm, lc):
    return (row0 + lax.broadcasted_iota(jnp.int32, (tm, 1), 0)) < lc


ROW_TILES = (768, 512, 384, 256, 128, 64, 32, 16, 8)
COL_TILES = (1024, 768, 512, 384, 256, 128)


def _matmul(name, mode, lhs, rhs, S=1, out_dtype=F32, bias=None, epilogue=None, extra=None, rider=None):
    if mode == "nn":
        M, K = lhs.shape
        Ns = rhs.shape[2]
        t0, t1, t2 = _pick(M, ROW_TILES), _pick(Ns, COL_TILES), _pick(K, (2048, 1024, 512, 256, 128))
        per = Ns // t1
        grid = (M // t0, rhs.shape[0] * per, K // t2)
        lhs_spec = pl.BlockSpec((t0, t2), lambda i, j, k: (i, k))
        rhs_spec = pl.BlockSpec((None, t2, t1), lambda i, j, k: (j // per, k, j % per))
        out_spec = pl.BlockSpec((t0, t1), lambda i, j, k: (i, j))
        out_shape = (M, rhs.shape[0] * Ns)
        contract = ((1,), (0,))
    elif mode == "nt":
        M = lhs.shape[0]
        _, K, Ns = rhs.shape
        t0, t1, t2 = _pick(M, ROW_TILES), _pick(K, (1024, 512, 256, 128)), _pick(Ns, COL_TILES)
        per = Ns // t2
        grid = (M // t0, K // t1, rhs.shape[0] * per)
        lhs_spec = pl.BlockSpec((t0, t2), lambda i, j, k: (i, k))
        rhs_spec = pl.BlockSpec((None, t1, t2), lambda i, j, k: (k // per, j, k % per))
        out_spec = pl.BlockSpec((t0, t1), lambda i, j, k: (i, j))
        out_shape = (M, K)
        contract = ((1,), (1,))
    else:
        M, K = lhs.shape
        Ns = rhs.shape[1] // S
        t0, t1, t2 = _pick(K, (1024, 512, 256, 128)), _pick(Ns, COL_TILES), _pick(M, ROW_TILES)
        per = Ns // t1
        grid = (K // t0, S * per, M // t2)
        lhs_spec = pl.BlockSpec((t2, t0), lambda i, j, k: (k, i))
        rhs_spec = pl.BlockSpec((t2, t1), lambda i, j, k: (k, j))
        out_spec = pl.BlockSpec((None, t0, t1), lambda i, j, k: (j // per, i, j % per))
        out_shape = (S, K, Ns)
        contract = ((0,), (0,))
    has_bias, has_extra, has_rider = bias is not None, extra is not None, rider is not None
    n_in = 2 + has_bias + has_extra + has_rider
    g0, g1, g2 = grid

    def body(*refs):
        l_ref, r_ref = refs[0], refs[1]
        bias_ref = refs[2] if has_bias else None
        e_ref = refs[2 + has_bias] if has_extra else None
        o_ref = refs[n_in]
        acc_ref = refs[n_in + 1 + has_rider]
        i, j, k = pl.program_id(0), pl.program_id(1), pl.program_id(2)
        if has_rider:
            start, mid, finish = RIDERS[rider[0]](refs[n_in - 1], refs[n_in + 1], *refs[n_in + 3:])
            pl.when((i == 0) & (j == 0) & (k == 0))(start)
            pl.when((i == g0 // 2) & (j == 0) & (k == 0))(mid)

        @pl.when(k == 0)
        def _():
            acc_ref[...] = jnp.zeros_like(acc_ref)

        acc_ref[...] += lax.dot_general(l_ref[...].astype(BF16), r_ref[...].astype(BF16), (contract, ((), ())),
                                        preferred_element_type=F32)

        @pl.when(k == g2 - 1)
        def _():
            acc = acc_ref[...]
            if has_bias:
                acc = acc + bias_ref[...]
            if epilogue is not None:
                acc = epilogue(acc, e_ref[...]) if has_extra else epilogue(acc)
            o_ref[...] = acc.astype(o_ref.dtype)

        if has_rider:
            pl.when((i == g0 - 1) & (j == g1 - 1) & (k == g2 - 1))(finish)

    in_specs, args = [lhs_spec, rhs_spec], [lhs, rhs]
    if has_bias:
        in_specs.append(pl.BlockSpec((1, out_spec.block_shape[-1]), lambda i, j, k: (0, j)))
        args.append(bias)
    if has_extra:
        in_specs.append(out_spec)
        args.append(extra)
    out_specs, out_shapes = [out_spec], [SDS(out_shape, out_dtype)]
    scratch = [pltpu.VMEM(tuple(d for d in out_spec.block_shape if d is not None), F32)]
    if has_rider:
        in_specs.append(pl.BlockSpec(memory_space=pl.ANY))
        args.append(rider[1])
        out_specs.append(pl.BlockSpec(memory_space=pl.ANY))
        out_shapes.append(_rider_out(*rider))
        scratch += _comm_scratch()
    res = pl.pallas_call(
        body, grid=grid, in_specs=in_specs, out_specs=out_specs, out_shape=out_shapes, scratch_shapes=scratch,
        name=name, compiler_params=_cp(("arbitrary", "arbitrary", "arbitrary")),
    )(*args)
    return (res[0], res[1]) if has_rider else res[0]


def _my_pos():
    return lax.axis_index("x"), lax.axis_index("y"), lax.axis_index("c")


def _flip(pos, k):
    x, y, c = pos
    return (1 - x if k & 4 else x, 1 - y if k & 2 else y, 1 - c if k & 1 else c)


def _lin(pos):
    return 4 * pos[0] + 2 * pos[1] + pos[2]


def _all_gather(name, x):
    r, c = x.shape

    def body(x_ref, out_ref, *sems):
        start, mid, finish = _gather_steps(x_ref, out_ref, *sems)
        start()
        mid()
        finish()

    return pl.pallas_call(
        body, out_shape=SDS((N_DEV, r, c), x.dtype),
        in_specs=[pl.BlockSpec(memory_space=pl.ANY)], out_specs=pl.BlockSpec(memory_space=pl.ANY),
        scratch_shapes=_comm_scratch(), name=name, compiler_params=_cp(),
    )(x)


def _comm_scratch():
    return [pltpu.SemaphoreType.DMA((7,)), pltpu.SemaphoreType.DMA((7,)), pltpu.SemaphoreType.DMA(())]


def _gather_steps(x_ref, out_ref, send_sems, recv_sems, local_sem):
    me = _my_pos()
    sibling = _flip(me, 1)
    chips = [2, 4, 6]

    def copy(k, block, to, src=None):
        dst = out_ref.at[_lin(block)]
        return pltpu.make_async_remote_copy(
            src_ref=dst if src is None else src, dst_ref=dst,
            send_sem=send_sems.at[k], recv_sem=recv_sems.at[k], device_id=to, device_id_type=MESH)

    def mine():
        return pltpu.make_async_copy(x_ref, out_ref.at[_lin(me)], local_sem)

    def first():
        return [copy(0, me, sibling, src=x_ref)] + [copy(1 + j, me, _flip(me, m), src=x_ref) for j, m in enumerate(chips)]

    def passed():
        return [copy(4 + j, _flip(me, m), sibling) for j, m in enumerate(chips)]

    def start():
        mine().start()
        for cp in first():
            cp.start()

    def mid():
        fw = passed()
        for j, m in enumerate(chips):
            copy(1 + j, _flip(me, m), me).wait_recv()
            fw[j].start()

    def finish():
        copy(0, sibling, me).wait_recv()
        for j, m in enumerate(chips):
            copy(4 + j, _flip(sibling, m), me).wait_recv()
        for cp in first() + passed():
            cp.wait_send()
        mine().wait()

    return start, mid, finish


def _exchange_steps(p_ref, out_ref, send_sems, recv_sems, local_sem):
    me = _my_pos()
    my_id = _lin(me)

    def mine():
        return pltpu.make_async_copy(p_ref.at[my_id], out_ref.at[my_id], local_sem)

    def sends():
        out = []
        for k in range(1, N_DEV):
            peer = _flip(me, k)
            out.append(pltpu.make_async_remote_copy(
                src_ref=p_ref.at[_lin(peer)], dst_ref=out_ref.at[my_id],
                send_sem=send_sems.at[k - 1], recv_sem=recv_sems.at[k - 1], device_id=peer, device_id_type=MESH))
        return out

    def start():
        mine().start()
        for cp in sends():
            cp.start()

    def mid():
        pass

    def finish():
        for k in range(1, N_DEV):
            peer = _flip(me, k)
            pltpu.make_async_remote_copy(
                src_ref=p_ref.at[my_id], dst_ref=out_ref.at[_lin(peer)],
                send_sem=send_sems.at[k - 1], recv_sem=recv_sems.at[k - 1], device_id=peer, device_id_type=MESH).wait_recv()
        for cp in sends():
            cp.wait_send()
        mine().wait()

    return start, mid, finish


RIDERS = {"gather": _gather_steps, "exchange": _exchange_steps}


def _rider_out(kind, payload):
    return SDS((N_DEV,) + tuple(payload.shape), payload.dtype) if kind == "gather" else SDS(payload.shape, payload.dtype)


def _exchange(name, p3):
    def body(p_ref, out_ref, *sems):
        start, mid, finish = _exchange_steps(p_ref, out_ref, *sems)
        start()
        finish()

    return pl.pallas_call(
        body, out_shape=SDS(p3.shape, p3.dtype),
        in_specs=[pl.BlockSpec(memory_space=pl.ANY)], out_specs=pl.BlockSpec(memory_space=pl.ANY),
        scratch_shapes=_comm_scratch(), name=name, compiler_params=_cp(),
    )(p3)


def _sum_blocks(name, g3):
    n, r, c = g3.shape
    tr = _pick(r, (512, 256, 128, 64, 32, 16, 8))
    tc = _pick(c, (2048, 1024, 512, 256, 128))

    def body(g_ref, o_ref):
        acc = g_ref[0].astype(F32)
        for d in range(1, n):
            acc = acc + g_ref[d].astype(F32)
        o_ref[...] = acc

    return pl.pallas_call(
        body, grid=(r // tr, c // tc),
        in_specs=[pl.BlockSpec((n, tr, tc), lambda i, j: (0, i, j))],
        out_specs=pl.BlockSpec((tr, tc), lambda i, j: (i, j)),
        out_shape=SDS((r, c), F32), name=name, compiler_params=_cp(("parallel", "parallel")),
    )(g3)


def _reduce_scatter(name, p3):
    return _sum_blocks(name + "_sum", _exchange(name + "_xchg", p3))


def _all_reduce(name, v):
    return _sum_blocks(name + "_sum", _all_gather(name + "_ag", v))


def _conv_geometry(K, lc):
    tb = _pick(lc, (256, 128, 64, 32, 16, 8))
    hal = -(-(K // 2) // SUBLANES) * SUBLANES
    return tb, hal


CONV_ROWS = 32


def _conv_layout(seg, tb, hal):
    segb = min(seg, tb)
    return segb, tb // segb, segb + 2 * hal


def _conv_scr_rows(tb, hal, seg_c, seg_l):
    return max(ns * stride for _, ns, stride in (_conv_layout(seg_c, tb, hal), _conv_layout(seg_l, tb, hal)))


def _conv_parts(i, tb, lc, seg_c, seg_l, part):
    if seg_c == seg_l:
        part(seg_c, 0)
        return

    @pl.when(i * tb < lc)
    def _():
        part(seg_c, 0)

    @pl.when(i * tb >= lc)
    def _():
        part(seg_l, lc)


def _conv_fill(scr, prev_ref, cur_ref, next_ref, i, tb, hal, seg, row0):
    segb, ns, stride = _conv_layout(seg, tb, hal)
    cb = cur_ref.shape[1]
    for s in range(ns):
        base = s * stride
        if seg > tb:
            at_start = ((i * tb - row0) & (seg - 1)) == 0
            at_end = ((i * tb + tb - row0) & (seg - 1)) == 0
            scr[pl.ds(base, hal), :] = jnp.where(at_start, 0.0, prev_ref[...])
            scr[pl.ds(base + hal + segb, hal), :] = jnp.where(at_end, 0.0, next_ref[...])
        else:
            scr[pl.ds(base, hal), :] = jnp.zeros((hal, cb), F32)
            scr[pl.ds(base + hal + segb, hal), :] = jnp.zeros((hal, cb), F32)
        scr[pl.ds(base + hal, segb), :] = cur_ref[pl.ds(s * segb, segb), :]


def _conv_tiles(tb, hal, seg):
    segb, ns, stride = _conv_layout(seg, tb, hal)
    rb = min(CONV_ROWS, segb)
    return [(s * stride + hal + r0, s * segb + r0, rb) for s in range(ns) for r0 in range(0, segb, rb)]


def _conv_specs(tb, hal, cb, nrb, col0):
    q = tb // hal
    nh = nrb * q
    return [pl.BlockSpec((hal, cb), lambda j, i: (jnp.maximum(i * q - 1, 0), col0 + j)),
            pl.BlockSpec((tb, cb), lambda j, i: (i, col0 + j)),
            pl.BlockSpec((hal, cb), lambda j, i: (jnp.minimum((i + 1) * q, nh - 1), col0 + j))]


def _conv_fwd(name, x, xcol0, C, w, b, lc, seg_c, seg_l, out=None, ocol0=0):
    R = x.shape[0]
    K = w.shape[0]
    tb, hal = _conv_geometry(K, lc)
    cb = _pick(C, (512, 256, 128))
    nrb = R // tb
    wp = jnp.zeros((-(-K // SUBLANES) * SUBLANES, C), F32).at[:K].set(w)

    def body(*refs):
        prev_ref, cur_ref, next_ref, w_ref, b_ref = refs[:5]
        o_ref, scr = refs[-2], refs[-1]
        i = pl.program_id(1)

        def part(seg, row0):
            _conv_fill(scr, prev_ref, cur_ref, next_ref, i, tb, hal, seg, row0)
            for src, dst, rb in _conv_tiles(tb, hal, seg):
                acc = jnp.broadcast_to(b_ref[...], (rb, cb))
                for k in range(K):
                    acc = acc + scr[pl.ds(src + k - K // 2, rb), :] * w_ref[pl.ds(k, 1), :]
                o_ref[pl.ds(dst, rb), :] = acc

        _conv_parts(i, tb, lc, seg_c, seg_l, part)

    in_specs = _conv_specs(tb, hal, cb, nrb, xcol0 // cb)
    in_specs += [pl.BlockSpec((wp.shape[0], cb), lambda j, i: (0, j)), pl.BlockSpec((1, cb), lambda j, i: (0, j))]
    args = [x, x, x, wp, b.reshape(1, C)]
    aliases = {}
    if out is not None:
        in_specs.append(pl.BlockSpec(memory_space=pl.ANY))
        args.append(out)
        aliases = {5: 0}
    oc0 = ocol0 // cb
    return pl.pallas_call(
        body, grid=(C // cb, nrb), in_specs=in_specs,
        out_specs=pl.BlockSpec((tb, cb), lambda j, i: (i, oc0 + j)),
        out_shape=SDS((R, out.shape[1] if out is not None else C), F32),
        scratch_shapes=[pltpu.VMEM((_conv_scr_rows(tb, hal, seg_c, seg_l), cb), F32)], input_output_aliases=aliases,
        name=name, compiler_params=_cp(("parallel", "arbitrary")),
    )(*args)


def _conv_bwd_w(name, x, xcol0, dy, w_shape, lc, seg_c, seg_l):
    R = x.shape[0]
    K, C = w_shape
    tb, hal = _conv_geometry(K, lc)
    cb = _pick(C, (512, 256, 128))
    nrb = R // tb
    kp = -(-(K + 1) // SUBLANES) * SUBLANES

    def body(prev_ref, cur_ref, next_ref, dy_ref, o_ref, scr, acc8):
        i = pl.program_id(1)

        @pl.when(i == 0)
        def _():
            acc8[...] = jnp.zeros_like(acc8)

        def fold(t):
            return jnp.sum(t.reshape(t.shape[0] // SUBLANES, SUBLANES, cb), axis=0)

        def part(seg, row0):
            _conv_fill(scr, prev_ref, cur_ref, next_ref, i, tb, hal, seg, row0)
            for src, dst, rb in _conv_tiles(tb, hal, seg):
                dy_t = dy_ref[pl.ds(dst, rb), :]
                for k in range(K):
                    acc8[pl.ds(SUBLANES * k, SUBLANES), :] += fold(scr[pl.ds(src + k - K // 2, rb), :] * dy_t)
                acc8[pl.ds(SUBLANES * K, SUBLANES), :] += fold(dy_t)

        _conv_parts(i, tb, lc, seg_c, seg_l, part)

        @pl.when(i == nrb - 1)
        def _():
            o_ref[...] = jnp.zeros_like(o_ref)
            for k in range(K + 1):
                o_ref[pl.ds(k, 1), :] = jnp.sum(acc8[pl.ds(SUBLANES * k, SUBLANES), :], axis=0, keepdims=True)

    in_specs = _conv_specs(tb, hal, cb, nrb, xcol0 // cb)
    in_specs.append(pl.BlockSpec((tb, cb), lambda j, i: (i, j)))
    res = pl.pallas_call(
        body, grid=(C // cb, nrb), in_specs=in_specs,
        out_specs=pl.BlockSpec((kp, cb), lambda j, i: (0, j)),
        out_shape=SDS((kp, C), F32),
        scratch_shapes=[pltpu.VMEM((_conv_scr_rows(tb, hal, seg_c, seg_l), cb), F32),
                        pltpu.VMEM((SUBLANES * (K + 1), cb), F32)],
        name=name, compiler_params=_cp(("parallel", "arbitrary")),
    )(x, x, x, dy)
    return res[:K], res[K]


def _bdot(dims):
    (ca, cb) = dims

    def raw(a, b, dn):
        return lax.dot_general(a.astype(BF16), b.astype(BF16), (dn, ((), ())), preferred_element_type=F32)

    @jax.custom_vjp
    def f(a, b):
        return raw(a, b, ((ca,), (cb,)))

    def fwd(a, b):
        return f(a, b), (a, b)

    def bwd(res, g):
        a, b = res
        if ca == 1:
            da = raw(g, b, ((1,), (1 - cb,)))
        else:
            da = raw(b, g, ((1 - cb,), (1,)))
        if cb == 0:
            db = raw(a, g, ((1 - ca,), (0,)))
        else:
            db = raw(g, a, ((0,), (1 - ca,)))
        return da, db

    f.defvjp(fwd, bwd)
    return f


_dot_nn = _bdot((1, 0))
_dot_nt = _bdot((1, 1))
_dot_tn = _bdot((0, 0))


def _split3(x):
    hi = x.astype(BF16)
    r1 = x - hi.astype(F32)
    mid = r1.astype(BF16)
    lo = (r1 - mid.astype(F32)).astype(BF16)
    return hi, mid, lo


def _tri_raw(tri, x):
    acc = None
    for part in _split3(x):
        t = jnp.dot(tri, part, preferred_element_type=F32)
        acc = t if acc is None else acc + t
    return acc


@jax.custom_vjp
def _tri_mm(tri, x):
    return _tri_raw(tri, x)


def _tri_mm_fwd(tri, x):
    return _tri_raw(tri, x), tri


def _tri_mm_bwd(tri, g):
    return jnp.zeros_like(tri), _tri_raw(tri.T, g)


_tri_mm.defvjp(_tri_mm_fwd, _tri_mm_bwd)


def _softplus(x):
    return jnp.maximum(x, 0.0) + jnp.log(1.0 + jnp.exp(-jnp.abs(x)))


def _dt_chunk(raw, bias, alog, nheads):
    q = raw.shape[0]
    dt = _softplus(raw + bias)
    dta = dt * (-jnp.exp(alog))
    li = lax.broadcasted_iota(jnp.int32, (q, q), 0)
    si = lax.broadcasted_iota(jnp.int32, (q, q), 1)
    lower = (si <= li).astype(BF16)
    upper = (si >= li).astype(BF16)
    col = lax.broadcasted_iota(jnp.int32, raw.shape, 1)
    cum = jnp.where(col < nheads, _tri_mm(lower, dta), _tri_mm(upper, dta))
    return dt.T, cum.T


def _dt_prep(name, proj, col_block, bias, alog, nheads):
    R = proj.shape[0]
    q = CHUNK

    def body(p_ref, b_ref, a_ref, dt_ref, cum_ref):
        dtt, cumt = _dt_chunk(p_ref[...], b_ref[...], a_ref[...], nheads)
        dt_ref[...] = dtt
        cum_ref[...] = cumt

    return pl.pallas_call(
        body, grid=(R // q,),
        in_specs=[pl.BlockSpec((q, LANES), lambda i: (i, col_block)),
                  pl.BlockSpec((1, LANES), lambda i: (0, 0)), pl.BlockSpec((1, LANES), lambda i: (0, 0))],
        out_specs=[pl.BlockSpec((LANES, q), lambda i: (0, i))] * 2,
        out_shape=[SDS((LANES, R), F32)] * 2, name=name, compiler_params=_cp(("parallel",)),
    )(proj, bias, alog)


def _dt_prep_bwd(name, proj, col_block, bias, alog, nheads, ddt, dcum, out, out_col_block):
    R = proj.shape[0]
    q = CHUNK
    wide = 4 * LANES

    def body(p_ref, b_ref, a_ref, g1_ref, g2_ref, out_any, o_ref, db_ref, da_ref):
        i = pl.program_id(0)
        _, vjp = jax.vjp(lambda r, b, a: _dt_chunk(r, b, a, nheads), p_ref[...], b_ref[...], a_ref[...])
        draw, db, da = vjp((g1_ref[...], g2_ref[...]))
        o_ref[...] = jnp.concatenate([draw, jnp.zeros((q, wide - LANES), F32)], axis=1)

        @pl.when(i == 0)
        def _():
            db_ref[...] = jnp.zeros_like(db_ref)
            da_ref[...] = jnp.zeros_like(da_ref)

        db_ref[...] += db
        da_ref[...] += da

    return pl.pallas_call(
        body, grid=(R // q,),
        in_specs=[pl.BlockSpec((q, LANES), lambda i: (i, col_block)),
                  pl.BlockSpec((1, LANES), lambda i: (0, 0)), pl.BlockSpec((1, LANES), lambda i: (0, 0)),
                  pl.BlockSpec((LANES, q), lambda i: (0, i)), pl.BlockSpec((LANES, q), lambda i: (0, i)),
                  pl.BlockSpec(memory_space=pl.ANY)],
        out_specs=[pl.BlockSpec((q, wide), lambda i: (i, out_col_block)),
                   pl.BlockSpec((1, LANES), lambda i: (0, 0)), pl.BlockSpec((1, LANES), lambda i: (0, 0))],
        out_shape=[SDS(out.shape, F32), SDS((1, LANES), F32), SDS((1, LANES), F32)],
        input_output_aliases={5: 0}, name=name, compiler_params=_cp(("arbitrary",)),
    )(proj, bias, alog, ddt, dcum, out)


def _ssd_chunk(rev, xpre, bpre, cpre, dt_rows, a_rows, hin, dvec):
    q, gw = xpre.shape
    hg = len(dt_rows)
    x, bm, cm = _silu(xpre), _silu(bpre), _silu(cpre)
    li = lax.broadcasted_iota(jnp.int32, (q, q), 0)
    si = lax.broadcasted_iota(jnp.int32, (q, q), 1)
    mask = (li <= si) if rev else (li >= si)
    lane = lax.broadcasted_iota(jnp.int32, (q, LANES), 1)
    lane1 = lax.broadcasted_iota(jnp.int32, (1, LANES), 1)
    laneq = lax.broadcasted_iota(jnp.int32, (1, q), 1)
    end = 0 if rev else q - 1
    scores = _dot_nt(cm, bm)
    dt_cols = [jnp.broadcast_to(r, (LANES, q)).T for r in dt_rows]
    a_cols = [jnp.broadcast_to(r, (LANES, q)).T for r in a_rows]
    tot = [jnp.sum(jnp.where(laneq == end, r, 0.0), axis=1, keepdims=True) for r in a_rows]
    pairs = hg // 2

    def expand(cols, n):
        return jnp.concatenate(
            [jnp.where((lane if n == q else lane1) < HEAD_P, cols[2 * p], cols[2 * p + 1]) for p in range(pairs)], axis=1)

    dt_exp = expand(dt_cols, q)
    a_exp = expand(a_cols, q)
    tot_exp = expand([jnp.broadcast_to(t, (1, LANES)) for t in tot], 1)
    xt = x * dt_exp
    ys = []
    for p in range(pairs):
        xp = xt[:, p * LANES:(p + 1) * LANES]
        acc = None
        for t in range(2):
            j = 2 * p + t
            seg = a_cols[j] - a_rows[j]
            m = scores * jnp.exp(jnp.where(mask, seg, -1e30))
            xm = jnp.where((lane < HEAD_P) if t == 0 else (lane >= HEAD_P), xp, 0.0)
            part = _dot_nn(m, xm)
            acc = part if acc is None else acc + part
        ys.append(acc)
    y_diag = jnp.concatenate(ys, axis=1)
    states = _dot_tn(bm, xt * jnp.exp(tot_exp - a_exp))
    hout = jnp.exp(tot_exp) * hin + states
    y_off = _dot_nn(cm, hin) * jnp.exp(a_exp)
    return y_diag + y_off + dvec * x, hout


def _chunk_order(rev, ncc, nct):
    if not rev:
        return lambda t: t
    return lambda t: jnp.where(t < ncc, ncc - 1 - t, nct - 1 - (t - ncc))


def _ssd_fwd(name, rev, xs, bs, cs, dtt, cumt, dvec, nheads, lc):
    R, DI = xs.shape
    q, n = CHUNK, N_STATE
    G = N_GROUPS
    gw = DI // G
    hg = gw // HEAD_P
    nct, ncc = R // q, lc // q
    order = _chunk_order(rev, ncc, nct)

    def body(x_ref, b_ref, c_ref, dt_ref, cum_ref, d_ref, y_ref, st_ref, h_scr):
        g, t = pl.program_id(0), pl.program_id(1)

        @pl.when(t == 0)
        def _():
            h_scr[...] = jnp.zeros_like(h_scr)

        base = (nheads if rev else 0) + g * hg
        dt_rows = [dt_ref[pl.ds(base + j, 1), :] for j in range(hg)]
        a_rows = [cum_ref[pl.ds(base + j, 1), :] for j in range(hg)]
        hin = h_scr[...]
        st_ref[...] = hin
        y, hout = _ssd_chunk(rev, x_ref[...], b_ref[...], c_ref[...], dt_rows, a_rows, hin, d_ref[...])
        y_ref[...] = y
        h_scr[...] = hout

    return pl.pallas_call(
        body, grid=(G, nct),
        in_specs=[pl.BlockSpec((q, gw), lambda g, t: (order(t), g)),
                  pl.BlockSpec((q, n), lambda g, t: (order(t), g)),
                  pl.BlockSpec((q, n), lambda g, t: (order(t), g)),
                  pl.BlockSpec((LANES, q), lambda g, t: (0, order(t))),
                  pl.BlockSpec((LANES, q), lambda g, t: (0, order(t))),
                  pl.BlockSpec((1, gw), lambda g, t: (0, g))],
        out_specs=[pl.BlockSpec((q, gw), lambda g, t: (order(t), g)),
                   pl.BlockSpec((None, None, n, gw), lambda g, t: (order(t), g, 0, 0))],
        out_shape=[SDS((R, DI), F32), SDS((nct, G, n, gw), F32)],
        scratch_shapes=[pltpu.VMEM((n, gw), F32)],
        name=name, compiler_params=_cp(("parallel", "arbitrary")),
    )(xs, bs, cs, dtt, cumt, dvec)


def _ssd_bwd(name, rev, xs, bs, cs, dtt, cumt, dvec, st, dy, nheads, lc, acc=None):
    R, DI = xs.shape
    q, n = CHUNK, N_STATE
    G = N_GROUPS
    gw = DI // G
    hg = gw // HEAD_P
    hgp = -(-hg // SUBLANES) * SUBLANES
    nct, ncc = R // q, lc // q
    fwd_order = _chunk_order(rev, ncc, nct)
    order = lambda t: fwd_order(nct - 1 - t)
    has_acc = acc is not None

    def body(*refs):
        x_ref, b_ref, c_ref, dt_ref, cum_ref, d_ref, st_ref, dy_ref = refs[:8]
        k = 8
        acc_refs = refs[k:k + 3] if has_acc else None
        k += 3 if has_acc else 0
        dx_ref, db_ref, dc_ref, ddt_ref, dcum_ref, dd_ref, dh_scr = refs[k:]
        g, t = pl.program_id(0), pl.program_id(1)

        @pl.when(t == 0)
        def _():
            dh_scr[...] = jnp.zeros_like(dh_scr)
            dd_ref[...] = jnp.zeros_like(dd_ref)

        base = (nheads if rev else 0) + g * hg
        dt_rows = [dt_ref[pl.ds(base + j, 1), :] for j in range(hg)]
        a_rows = [cum_ref[pl.ds(base + j, 1), :] for j in range(hg)]
        _, vjp = jax.vjp(functools.partial(_ssd_chunk, rev), x_ref[...], b_ref[...], c_ref[...],
                         dt_rows, a_rows, st_ref[...], d_ref[...])
        dx, db, dc, ddt, da, dhin, dd = vjp((dy_ref[...], dh_scr[...]))
        if has_acc:
            dx, db, dc = dx + acc_refs[0][...], db + acc_refs[1][...], dc + acc_refs[2][...]
        dx_ref[...] = dx
        db_ref[...] = db
        dc_ref[...] = dc
        if hgp > hg:
            ddt_ref[...] = jnp.zeros_like(ddt_ref)
            dcum_ref[...] = jnp.zeros_like(dcum_ref)
        for j in range(hg):
            ddt_ref[pl.ds(j, 1), :] = ddt[j]
            dcum_ref[pl.ds(j, 1), :] = da[j]
        dd_ref[...] += dd
        dh_scr[...] = dhin

    row_specs = [pl.BlockSpec((q, gw), lambda g, t: (order(t), g)),
                 pl.BlockSpec((q, n), lambda g, t: (order(t), g)),
                 pl.BlockSpec((q, n), lambda g, t: (order(t), g))]
    in_specs = row_specs + [
        pl.BlockSpec((LANES, q), lambda g, t: (0, order(t))),
        pl.BlockSpec((LANES, q), lambda g, t: (0, order(t))),
        pl.BlockSpec((1, gw), lambda g, t: (0, g)),
        pl.BlockSpec((None, None, n, gw), lambda g, t: (order(t), g, 0, 0)),
        pl.BlockSpec((q, gw), lambda g, t: (order(t), g))]
    args = [xs, bs, cs, dtt, cumt, dvec, st, dy]
    aliases = {}
    if has_acc:
        in_specs += row_specs
        args += list(acc)
        aliases = {8: 0, 9: 1, 10: 2}
    return pl.pallas_call(
        body, grid=(G, nct), in_specs=in_specs,
        out_specs=row_specs + [pl.BlockSpec((None, hgp, q), lambda g, t: (g, 0, order(t))),
                               pl.BlockSpec((None, hgp, q), lambda g, t: (g, 0, order(t))),
                               pl.BlockSpec((1, gw), lambda g, t: (0, g))],
        out_shape=[SDS(xs.shape, F32), SDS(bs.shape, F32), SDS(cs.shape, F32),
                   SDS((G, hgp, R), F32), SDS((G, hgp, R), F32), SDS((1, DI), F32)],
        scratch_shapes=[pltpu.VMEM((n, gw), F32)], input_output_aliases=aliases,
        name=name, compiler_params=_cp(("parallel", "arbitrary")),
    )(*args)


def _row_tile(width):
    return 256 if width <= 2048 else 128


def _rowwise_op(name, f, n_rows, out_widths):
    def fwd(*args):
        rows, vecs = args[:n_rows], args[n_rows:]
        R = rows[0].shape[0]
        tm = min(R, _row_tile(max([a.shape[1] for a in rows] + list(out_widths))))
        outs = _rowwise(name, lambda r0, rt, vt: (f(r0, rt, vt), []), R, tm,
                        [(a, a.shape[1], 0) for a in rows], list(vecs),
                        [(w, F32, None, 0, None) for w in out_widths])
        return (outs[0] if len(outs) == 1 else tuple(outs)), args

    def bwd(args, ct):
        rows, vecs = args[:n_rows], args[n_rows:]
        cts = [ct] if len(out_widths) == 1 else list(ct)
        R = rows[0].shape[0]
        tm = min(R, _row_tile(max([a.shape[1] for a in rows] + list(out_widths))))
        fb = _vjp_fn(f, n_rows, [True] * n_rows, [True] * len(vecs))
        res = _rowwise(name + "_bwd", fb, R, tm, [(a, a.shape[1], 0) for a in list(rows) + cts], list(vecs),
                       [(a.shape[1], F32, None, 0, None) for a in rows], [v.shape for v in vecs])
        return tuple(res)

    return _cvjp(fwd, bwd)


def _norm_mod_op(name, lc):
    def f(row0, rows, vecs):
        (h,), (g, shc, scc, shl, scl) = rows, vecs
        ctx = _is_ctx(row0, h.shape[0], lc)
        return [_rms(h, g) * (1.0 + jnp.where(ctx, scc, scl)) + jnp.where(ctx, shc, shl)]

    def fwd(h, *vecs):
        R, D = h.shape
        (u,) = _rowwise(name, lambda r0, rt, vt: (f(r0, rt, vt), []), R, min(R, _row_tile(D)),
                        [(h, D, 0)], list(vecs), [(D, F32, None, 0, None)])
        return (u, h), (h,) + vecs

    def bwd(res, ct):
        h, vecs = res[0], res[1:]
        du, dhp = ct
        R, D = h.shape
        fb = _vjp_fn(f, 1, [True], [True] * 5, add_last=True)
        out = _rowwise(name + "_bwd", fb, R, min(R, _row_tile(D)), [(h, D, 0), (du, D, 0), (dhp, D, 0)], list(vecs),
                       [(D, F32, None, 0, None)], [v.shape for v in vecs])
        return tuple(out)

    return _cvjp(fwd, bwd)


def _post_res_op(name, lc):
    def branch(row0, rows, vecs):
        (y,), (g, gc, gl, bias) = rows, vecs
        return [jnp.where(_is_ctx(row0, y.shape[0], lc), gc, gl) * _rms(y + bias, g)]

    def full(row0, rows, vecs):
        return [rows[0] + branch(row0, rows[1:], vecs)[0]]

    def fwd(h, y, *vecs):
        R, D = h.shape
        (out,) = _rowwise(name, lambda r0, rt, vt: (full(r0, rt, vt), []), R, min(R, _row_tile(D)),
                          [(h, D, 0), (y, D, 0)], list(vecs), [(D, F32, None, 0, None)])
        return out, (y,) + vecs

    def bwd(res, d):
        y, vecs = res[0], res[1:]
        R, D = y.shape
        fb = _vjp_fn(branch, 1, [True], [True] * 4)
        out = _rowwise(name + "_bwd", fb, R, min(R, _row_tile(D)), [(y, D, 0), (d, D, 0)], list(vecs),
                       [(D, F32, None, 0, None)], [v.shape for v in vecs])
        return (d,) + tuple(out)

    return _cvjp(fwd, bwd)


def _payload(w, wd):
    layout, (k, ns), _ = wd
    wb = w.astype(BF16)
    return wb.reshape(k * ns // LANES, LANES) if layout == "flat" else wb


def _as_b3(g, wd):
    layout, (k, ns), npad = wd
    if layout == "col":
        return g
    if layout == "row":
        return g.reshape(1, N_DEV * k, ns)
    full = jnp.transpose(g.reshape(N_DEV, k, ns), (1, 0, 2)).reshape(k, N_DEV * ns)
    return jnp.pad(full, ((0, 0), (0, npad - N_DEV * ns)))[None]


def _as_blocks(dw3, wd):
    layout, (k, ns), _ = wd
    if layout == "col":
        return dw3
    if layout == "row":
        return dw3.reshape(N_DEV, k, ns)
    p = dw3[0, :, :N_DEV * ns].reshape(k, N_DEV, ns)
    return jnp.transpose(p, (1, 0, 2)).reshape(N_DEV, k * ns // LANES, LANES)


def _n_shards(wd):
    return N_DEV if wd[0] == "col" else 1


def _head_op(name, wd):
    def fwd(w):
        return _as_b3(_all_gather(name, _payload(w, wd)), wd), ()

    def bwd(_, db):
        return (_reduce_scatter(name + "_rs", _as_blocks(db, wd)).reshape(wd[1]),)

    return _cvjp(fwd, bwd)


def _chain_op(name, wd_cur, wd_next):
    def fwd(a, b, wn):
        out, g = _matmul(name, "nn", a, b, rider=("gather", _payload(wn, wd_next)))
        return (out, _as_b3(g, wd_next)), (a, b)

    def bwd(res, ct):
        a, b = res
        dout, dbn = ct
        da, recv = _matmul(name + "_da", "nt", dout, b, rider=("exchange", _as_blocks(dbn, wd_next)))
        dw = _matmul(name + "_dw", "tn", a, dout, S=_n_shards(wd_cur), out_dtype=BF16)
        return da, dw, _sum_blocks(name + "_rsum", recv).reshape(wd_next[1])

    return _cvjp(fwd, bwd)


def _mlp_op(name, wd2, wd_next):
    has_next = wd_next is not None

    def fwd(u, b1, w2, *wn):
        act, g2 = _matmul(name + "_up", "nn", u, b1, out_dtype=BF16,
                          epilogue=lambda acc: jnp.square(jnp.maximum(acc, 0.0)), rider=("gather", _payload(w2, wd2)))
        b2 = _as_b3(g2, wd2)
        if has_next:
            f, gn = _matmul(name + "_down", "nn", act, b2, rider=("gather", _payload(wn[0], wd_next)))
            return (f, _as_b3(gn, wd_next)), (u, act, b1, b2)
        return _matmul(name + "_down_last", "nn", act, b2), (u, act, b1, b2)

    def bwd(res, ct):
        u, act, b1, b2 = res
        dact = dict(out_dtype=BF16, extra=act, epilogue=lambda acc, a: acc * (2.0 * jnp.sqrt(a.astype(F32))))
        if has_next:
            df, dbn = ct
            da, recvn = _matmul(name + "_dact", "nt", df, b2, rider=("exchange", _as_blocks(dbn, wd_next)), **dact)
        else:
            df = ct
            da = _matmul(name + "_dact_last", "nt", df, b2, **dact)
        dw2 = _matmul(name + "_dw2", "tn", act, df, S=1, out_dtype=BF16)
        dw1, recv2 = _matmul(name + "_dw1", "tn", u, da, S=N_DEV, out_dtype=BF16,
                             rider=("exchange", _as_blocks(dw2, wd2)))
        du = _matmul(name + "_du", "nt", da, b1)
        out = (du, dw1, _sum_blocks(name + "_rsum2", recv2).reshape(wd2[1]))
        if has_next:
            out += (_sum_blocks(name + "_rsumn", recvn).reshape(wd_next[1]),)
        return out

    return _cvjp(fwd, bwd)


def _conv_op(name, lc, seg_c, seg_l):
    def fwd(v, w, b):
        return _conv_fwd(name, v, 0, v.shape[1], w, b, lc, seg_c, seg_l), (v, w)

    def bwd(res, dy):
        v, w = res
        C = v.shape[1]
        dv = _conv_fwd(name + "_dx", dy, 0, C, w[::-1], jnp.zeros((C,), F32), lc, seg_c, seg_l)
        dw, db = _conv_bwd_w(name + "_dw", v, 0, dy, w.shape, lc, seg_c, seg_l)
        return dv, dw, db

    return _cvjp(fwd, bwd)


def _ssm_core_op(name, cfg):
    DI, GN, H, lc, L, NP = cfg["DI"], cfg["GN"], cfg["H"], cfg["Lc"], cfg["L"], cfg["NP"]
    parts = [(DI, DI), (2 * DI, GN), (2 * DI + GN, GN)]
    dt_col = 2 * DI + 2 * GN

    def gated(row0, rows, vecs):
        (yf, yb, z), (g,) = rows, vecs
        return [_rms((yf + yb) * _silu(z), g)]

    def fwd(proj, cw, cb, alog, bias, d_f, d_b, ng):
        R = proj.shape[0]
        pre, off = [], 0
        for idx, (col, wd) in enumerate(parts):
            pre.append(_conv_fwd(f"{name}_conv{idx}", proj, col, wd, cw[:, off:off + wd], cb[off:off + wd], lc, lc, L))
            off += wd
        dtt, cumt = _dt_prep(name + "_dt", proj, dt_col // LANES, bias, alog, H)
        yf, stf = _ssd_fwd(name + "_scan_f", False, *pre, dtt, cumt, d_f, H, lc)
        yb, stb = _ssd_fwd(name + "_scan_b", True, *pre, dtt, cumt, d_b, H, lc)
        tm = min(R, _row_tile(DI))
        (out,) = _rowwise(name + "_gate", lambda r0, rt, vt: (gated(r0, rt, vt), []), R, tm,
                          [(yf, DI, 0), (yb, DI, 0), (proj, DI, 0)], [ng], [(DI, F32, None, 0, None)])
        return out, (proj, cw, alog, bias, d_f, d_b, ng, pre, dtt, cumt, yf, yb, stf, stb)

    def bwd(res, dout):
        proj, cw, alog, bias, d_f, d_b, ng, pre, dtt, cumt, yf, yb, stf, stb = res
        R = proj.shape[0]
        tm = min(R, _row_tile(DI))
        fb = _vjp_fn(gated, 3, [True, False, True], [True])
        dy, dproj, dng = _rowwise(name + "_gate_bwd", fb, R, tm,
                                  [(yf, DI, 0), (yb, DI, 0), (proj, DI, 0), (dout, DI, 0)], [ng],
                                  [(DI, F32, None, 0, None), (DI, F32, NP, 0, None)], [ng.shape])
        rf = _ssd_bwd(name + "_scan_f_bwd", False, *pre, dtt, cumt, d_f, stf, dy, H, lc)
        rb = _ssd_bwd(name + "_scan_b_bwd", True, *pre, dtt, cumt, d_b, stb, dy, H, lc, acc=rf[:3])
        hg = DI // N_GROUPS // HEAD_P

        def head_rows(t3):
            return t3[:, :hg].reshape(H, R)

        zpad = jnp.zeros((LANES - 2 * H, R), F32)
        ddt = jnp.concatenate([head_rows(rf[3]), head_rows(rb[3]), zpad], axis=0)
        dcum = jnp.concatenate([head_rows(rf[4]), head_rows(rb[4]), zpad], axis=0)
        dproj, dbias, dalog = _dt_prep_bwd(name + "_dt_bwd", proj, dt_col // LANES, bias, alog, H, ddt, dcum,
                                           dproj, dt_col // (4 * LANES))
        dws, dbs, off = [], [], 0
        for idx, (col, wd) in enumerate(parts):
            dpart = rb[idx]
            dproj = _conv_fwd(f"{name}_conv{idx}_dx", dpart, 0, wd, cw[::-1, off:off + wd], jnp.zeros((wd,), F32),
                              lc, lc, L, out=dproj, ocol0=col)
            dw, db = _conv_bwd_w(f"{name}_conv{idx}_dw", proj, col, dpart, (cw.shape[0], wd), lc, lc, L)
            dws.append(dw)
            dbs.append(db)
            off += wd
        return (dproj, jnp.concatenate(dws, axis=1), jnp.concatenate(dbs), dalog, dbias, rf[5], rb[5], dng)

    return _cvjp(fwd, bwd)


def _mod_op(name):
    def fwd(s8, w, b):
        depth, d, ns = w.shape
        me = _lin(_my_pos())
        rows = N_DEV * SUBLANES
        s_all = _all_gather(name + "_s", s8).reshape(rows, d)
        wb = w.astype(BF16)
        bcols = lax.dynamic_slice_in_dim(b, me * ns, ns, axis=1)
        part = jnp.concatenate([_matmul(name + "_mm", "nn", s_all, wb[i][None], bias=bcols[i][None])
                                for i in range(depth)], axis=1)
        allm = _all_gather(name + "_m", part)
        mine = lax.dynamic_slice_in_dim(allm, me * SUBLANES, SUBLANES, axis=1)
        m = jnp.transpose(mine.reshape(N_DEV, SUBLANES, depth, ns), (2, 1, 0, 3)).reshape(depth, SUBLANES, N_DEV * ns)
        return m, (s_all, wb)

    def bwd(res, dm):
        s_all, wb = res
        depth, d, ns = wb.shape
        me = _lin(_my_pos())
        rows = N_DEV * SUBLANES
        fac = jnp.transpose(dm, (1, 0, 2)).reshape(SUBLANES, depth * N_DEV * ns)
        allf = _all_gather(name + "_fac", fac).reshape(rows, depth, N_DEV, ns)
        mine = lax.dynamic_slice_in_dim(allf, me, 1, axis=2)[:, :, 0]
        dws = [_matmul(name + "_dw", "tn", s_all, mine[:, i], S=1, out_dtype=F32)[0] for i in range(depth)]
        ds_part = sum(_matmul(name + "_ds", "nt", mine[:, i], wb[i][None]) for i in range(depth))
        ds_all = _all_reduce(name + "_dsr", ds_part)
        ds = lax.dynamic_slice_in_dim(ds_all, me * SUBLANES, SUBLANES, axis=0)
        db = _sum_blocks(name + "_db", allf.reshape(rows, depth * N_DEV * ns // LANES, LANES))
        return ds, jnp.stack(dws), db.reshape(depth, N_DEV * ns)

    return _cvjp(fwd, bwd)


def _loss_grad(name, h, target, lc):
    R, D = h.shape
    tm = min(lc, 256)
    off = lc // tm

    def body(h_ref, t_ref, g_ref, s_ref):
        i = pl.program_id(0)

        @pl.when(i == 0)
        def _():
            s_ref[...] = jnp.zeros_like(s_ref)

        err = jnp.where(i >= off, h_ref[...] - t_ref[...], 0.0)
        g_ref[...] = err * (1.0 / D)
        s_ref[...] += jnp.sum(err * err, axis=0, keepdims=True)

    return pl.pallas_call(
        body, grid=(R // tm,),
        in_specs=[pl.BlockSpec((tm, D), lambda i: (i, 0)),
                  pl.BlockSpec((tm, D), lambda i: (jnp.maximum(i - off, 0), 0))],
        out_specs=[pl.BlockSpec((tm, D), lambda i: (i, 0)), pl.BlockSpec((1, D), lambda i: (0, 0))],
        out_shape=[SDS((R, D), F32), SDS((1, D), F32)], name=name, compiler_params=_cp(("arbitrary",)),
    )(h, target)


def _adam(name, w, g, m, v):
    shape = w.shape
    c = shape[-1]
    r = w.size // c
    tr = _pick(r, (256, 128, 64, 32, 16, 8))
    tc = _pick(c, (2048, 1536, 1024, 768, 512, 256, 128))
    c1 = 1.0 / (1.0 - ADAM_B1 ** ADAM_STEP)
    c2 = 1.0 / (1.0 - ADAM_B2 ** ADAM_STEP)

    def body(w_ref, g_ref, m_ref, v_ref, d_ref, mo_ref, vo_ref):
        gg = g_ref[...]
        mn = ADAM_B1 * m_ref[...] + (1.0 - ADAM_B1) * gg
        vn = ADAM_B2 * v_ref[...] + (1.0 - ADAM_B2) * jnp.square(gg)
        d_ref[...] = -ADAM_LR * ((mn * c1) / (jnp.sqrt(vn * c2) + ADAM_EPS) + ADAM_WD * w_ref[...])
        mo_ref[...] = mn
        vo_ref[...] = vn

    spec = pl.BlockSpec((tr, tc), lambda i, j: (i, j))
    res = pl.pallas_call(
        body, grid=(r // tr, c // tc), in_specs=[spec] * 4, out_specs=[spec] * 3,
        out_shape=[SDS((r, c), F32)] * 3, name=name, compiler_params=_cp(("parallel", "parallel")),
    )(*[a.reshape(r, c) for a in (w, g, m, v)])
    return [a.reshape(shape) for a in res]


WEIGHTS = ["c_ctx", "mod_w", "mod_b", "pre_mix_g", "post_mix_g", "pre_mlp_g", "post_mlp_g", "mlp_w1", "mlp_w2",
           "ssm_in_w", "ssm_conv_w", "ssm_conv_b", "ssm_a_log_f", "ssm_dt_bias_f", "ssm_d_f", "ssm_a_log_b",
           "ssm_dt_bias_b", "ssm_d_b", "ssm_norm_g", "ssm_out_w", "conf_pw1_w", "conf_pw1_b", "conf_dw_w",
           "conf_dw_b", "conf_ln_g", "conf_ln_b", "conf_pw2_w", "conf_pw2_b"]
BIG_SHARDED = ["mod_w", "mlp_w1", "mlp_w2", "ssm_in_w", "ssm_out_w", "conf_pw1_w", "conf_pw2_w"]
SMALL_SHARDED = {"ssm_conv_w": 2, "conf_pw1_b": 1, "conf_dw_w": 2, "conf_dw_b": 1, "conf_ln_g": 1, "conf_ln_b": 1,
                 "conf_pw2_b": 1}


def _pack(arrs):
    flat = jnp.concatenate([a.reshape(-1).astype(F32) for a in arrs])
    n = flat.shape[0]
    tile = LANES * SUBLANES
    npad = -(-n // tile) * tile
    return jnp.pad(flat, (0, npad - n)).reshape(npad // LANES, LANES)


def _unpack(flat, shapes):
    out, off = [], 0
    for s in shapes:
        n = 1
        for d in s:
            n *= d
        out.append(flat[off:off + n].reshape(s))
        off += n
    return out


def _to_col(h, lc, rows_g):
    d = h.shape[1]
    lat = h[lc:].reshape(rows_g, GRID_W, d).swapaxes(0, 1).reshape(-1, d)
    return jnp.concatenate([h[:lc], lat], axis=0)


def _from_col(h, lc, rows_g):
    d = h.shape[1]
    lat = h[lc:].reshape(GRID_W, rows_g, d).swapaxes(0, 1).reshape(-1, d)
    return jnp.concatenate([h[:lc], lat], axis=0)


def _step(a):
    x, ctx, target = a["x"][0], a["ctx"][0], a["loss_target"][0]
    L, D = x.shape
    lc = ctx.shape[0]
    depth = a["mod_w"].shape[0]
    DI = 2 * D
    H = DI // HEAD_P
    GN = N_GROUPS * N_STATE
    cfg = dict(DI=DI, GN=GN, H=H, Lc=lc, L=L, NP=2 * DI + 2 * GN + 4 * LANES)
    rows_g = L // GRID_W
    me = _lin(_my_pos())

    names = list(SMALL_SHARDED)
    gathered = _all_gather("small_w_ag", _pack([a[n] for n in names])).reshape(N_DEV, -1)
    full_small, off = {}, 0
    for n in names:
        shp, ax = a[n].shape, SMALL_SHARDED[n]
        seg = gathered[:, off:off + a[n].size].reshape((N_DEV,) + shp)
        off += a[n].size
        full_small[n] = jnp.moveaxis(seg, 0, ax).reshape(shp[:ax] + (N_DEV * shp[ax],) + shp[ax + 1:])

    params = {n: (full_small[n] if n in SMALL_SHARDED else a[n]) for n in WEIGHTS}
    params["x"] = x

    def trunk(p):
        raw8 = jnp.concatenate([a["c"], p["c_ctx"][None], jnp.zeros((SUBLANES - 2, D), F32)], axis=0)
        s8 = _rowwise_op("silu_c", lambda r0, rows, vecs: [_silu(rows[0])], 1, [D])(raw8)
        mods = _mod_op("mod")(s8, p["mod_w"], p["mod_b"])
        zero_bias = jnp.zeros((1, D), F32)
        h = jnp.concatenate([ctx, p["x"]], axis=0)

        seq = []
        for i in range(depth):
            j = i // 2
            mixer = [("in", "ssm_in_w", "flat"), ("out", "ssm_out_w", "row")] if i % 2 == 0 else \
                    [("pw1", "conf_pw1_w", "col"), ("pw2", "conf_pw2_w", "row")]
            for role, n, layout in mixer:
                seq.append((role, p[n][j], (layout, p[n][j].shape, cfg["NP"])))
            seq.append(("w1", p["mlp_w1"][i], ("col", p["mlp_w1"][i].shape, 0)))
            seq.append(("w2", p["mlp_w2"][i], ("row", p["mlp_w2"][i].shape, 0)))
        state = {"t": 0, "b": _head_op("head_ag", seq[0][2])(seq[0][1])}

        def mm(a):
            t = state["t"]
            out, state["b"] = _chain_op(seq[t][0], seq[t][2], seq[t + 1][2])(a, state["b"], seq[t + 1][1])
            state["t"] = t + 1
            return out

        def mlp(u2):
            t = state["t"]
            if t + 2 < len(seq):
                f, state["b"] = _mlp_op("mlp_" + seq[t + 2][0], seq[t + 1][2], seq[t + 2][2])(
                    u2, state["b"], seq[t + 1][1], seq[t + 2][1])
            else:
                f = _mlp_op("mlp", seq[t + 1][2], None)(u2, state["b"], seq[t + 1][1])
            state["t"] = t + 2
            return f

        for i in range(depth):
            kind, j = i % 2, i // 2
            col_major = (j % 2) == 1
            if i == 2:
                h = _to_col(h, lc, rows_g)
            sh1, sc1, g1, sh2, sc2, g2 = jnp.split(mods[i], 6, axis=1)
            u, hp = _norm_mod_op("pre_norm", lc)(h, p["pre_mix_g"][i][None], sh1[1:2], sc1[1:2], sh1[0:1], sc1[0:1])
            if kind == 0:
                proj = mm(u)
                pad = jnp.zeros((LANES - 2 * H,), F32)
                alog = jnp.concatenate([p["ssm_a_log_f"][j], p["ssm_a_log_b"][j], pad])[None]
                bias = jnp.concatenate([p["ssm_dt_bias_f"][j], p["ssm_dt_bias_b"][j], pad])[None]
                y = _ssm_core_op("ssm", cfg)(
                    proj, p["ssm_conv_w"][j], p["ssm_conv_b"][j], alog, bias,
                    jnp.repeat(p["ssm_d_f"][j], HEAD_P)[None], jnp.repeat(p["ssm_d_b"][j], HEAD_P)[None],
                    p["ssm_norm_g"][j][None])
                o = mm(y)
                mix_bias = zero_bias
            else:
                pre = mm(u)
                glu = _rowwise_op("glu", lambda r0, rows, vecs: [
                    (rows[0] + vecs[0])[:, :D] * jax.nn.sigmoid((rows[0] + vecs[0])[:, D:])], 1, [D])
                v = glu(pre, p["conf_pw1_b"][j][None])
                seg = rows_g if col_major else GRID_W
                cv = _conv_op(f"dw{seg}", lc, lc, seg)(v, p["conf_dw_w"][j], p["conf_dw_b"][j])

                def ln_swish(r0, rows, vecs):
                    xc = rows[0] - jnp.mean(rows[0], axis=-1, keepdims=True)
                    yn = xc * lax.rsqrt(jnp.mean(xc * xc, axis=-1, keepdims=True) + EPS) * vecs[0] + vecs[1]
                    return [_silu(yn)]

                w = _rowwise_op("ln_swish", ln_swish, 1, [D])(cv, p["conf_ln_g"][j][None], p["conf_ln_b"][j][None])
                o = mm(w)
                mix_bias = p["conf_pw2_b"][j][None]
            h = _post_res_op("post_norm", lc)(hp, o, p["post_mix_g"][i][None], g1[1:2], g1[0:1], mix_bias)
            u2, hp = _norm_mod_op("pre_norm", lc)(h, p["pre_mlp_g"][i][None], sh2[1:2], sc2[1:2], sh2[0:1], sc2[0:1])
            f = mlp(u2)
            h = _post_res_op("post_norm", lc)(hp, f, p["post_mlp_g"][i][None], g2[1:2], g2[0:1], zero_bias)
        return _from_col(h, lc, rows_g) if depth > 2 else h

    h_out, vjp = jax.vjp(trunk, params)
    dh, sq = _loss_grad("loss", h_out, target, lc)
    (grads,) = vjp(dh)
    loss = lax.psum(jnp.sum(sq) * (0.5 / D), ("x", "y", "c"))

    part_names = [n for n in WEIGHTS if n not in BIG_SHARDED and n != "mod_b"]
    red = _all_reduce("small_g", _pack([grads[n] for n in part_names])).reshape(-1)
    reduced = dict(zip(part_names, _unpack(red, [grads[n].shape for n in part_names])))
    final = {}
    for n in WEIGHTS:
        if n in BIG_SHARDED or n == "mod_b":
            final[n] = grads[n]
        elif n in SMALL_SHARDED:
            ax = SMALL_SHARDED[n]
            final[n] = lax.dynamic_slice_in_dim(reduced[n], me * a[n].shape[ax], a[n].shape[ax], axis=ax)
        else:
            final[n] = reduced[n]

    deltas, new_m, new_v = [], [], []
    for n in WEIGHTS:
        d_, m_, v_ = _adam("adam_" + n, a[n], final[n], a["m_" + n], a["v_" + n])
        deltas.append(d_)
        new_m.append(m_)
        new_v.append(v_)
    return (loss, grads["x"][None], *[final[n] for n in WEIGHTS], *deltas, *new_m, *new_v)


def kernel(x, c, ctx, c_ctx, mod_w, mod_b, pre_mix_g, post_mix_g, pre_mlp_g, post_mlp_g, mlp_w1, mlp_w2, ssm_in_w, ssm_conv_w, ssm_conv_b, ssm_a_log_f, ssm_dt_bias_f, ssm_d_f, ssm_a_log_b, ssm_dt_bias_b, ssm_d_b, ssm_norm_g, ssm_out_w, conf_pw1_w, conf_pw1_b, conf_dw_w, conf_dw_b, conf_ln_g, conf_ln_b, conf_pw2_w, conf_pw2_b, loss_target, m_c_ctx, m_mod_w, m_mod_b, m_pre_mix_g, m_post_mix_g, m_pre_mlp_g, m_post_mlp_g, m_mlp_w1, m_mlp_w2, m_ssm_in_w, m_ssm_conv_w, m_ssm_conv_b, m_ssm_a_log_f, m_ssm_dt_bias_f, m_ssm_d_f, m_ssm_a_log_b, m_ssm_dt_bias_b, m_ssm_d_b, m_ssm_norm_g, m_ssm_out_w, m_conf_pw1_w, m_conf_pw1_b, m_conf_dw_w, m_conf_dw_b, m_conf_ln_g, m_conf_ln_b, m_conf_pw2_w, m_conf_pw2_b, v_c_ctx, v_mod_w, v_mod_b, v_pre_mix_g, v_post_mix_g, v_pre_mlp_g, v_post_mlp_g, v_mlp_w1, v_mlp_w2, v_ssm_in_w, v_ssm_conv_w, v_ssm_conv_b, v_ssm_a_log_f, v_ssm_dt_bias_f, v_ssm_d_f, v_ssm_a_log_b, v_ssm_dt_bias_b, v_ssm_d_b, v_ssm_norm_g, v_ssm_out_w, v_conf_pw1_w, v_conf_pw1_b, v_conf_dw_w, v_conf_dw_b, v_conf_ln_g, v_conf_ln_b, v_conf_pw2_w, v_conf_pw2_b):
    return _step(dict(locals()))
```

```python
import functools

import jax
import jax.numpy as jnp
from jax import lax
from jax.experimental import pallas as pl
from jax.experimental.pallas import tpu as pltpu

F32, BF16 = jnp.float32, jnp.bfloat16
SDS = jax.ShapeDtypeStruct

GRID_W = 64
HEAD_P = 64
N_GROUPS = 8
N_STATE = 128
SSM_K = 5
CHUNK = 128
CONF_K = 31
EPS = 1e-6
ADAM_LR, ADAM_B1, ADAM_B2, ADAM_EPS, ADAM_WD, ADAM_STEP = 0.001, 0.9, 0.999, 1e-08, 0.01, 10

LANES = 128
SUBLANES = 8
N_DEV = 8
MESH = pl.DeviceIdType.MESH
VMEM_LIMIT = 56 * 2**20


def _cp(sem=None, vmem=VMEM_LIMIT):
    if sem is None:
        return pltpu.CompilerParams(vmem_limit_bytes=vmem)
    return pltpu.CompilerParams(dimension_semantics=sem, vmem_limit_bytes=vmem)


def _pick(dim, cands):
    for c in cands:
        if c <= dim and dim % c == 0:
            return c
    return dim


def _cvjp(fwd, bwd):
    @jax.custom_vjp
    def op(*args):
        return fwd(*args)[0]
    op.defvjp(fwd, bwd)
    return op


def _rowwise(name, fn, R, tm, rows, vecs, outs, vouts=()):
    n_r, n_v, n_o, n_vo = len(rows), len(vecs), len(outs), len(vouts)
    alias_in = [o[4] for o in outs if o[4] is not None]
    n_a = len(alias_in)

    def body(*refs):
        r = refs[:n_r]
        v = refs[n_r:n_r + n_v]
        o = refs[n_r + n_v + n_a:n_r + n_v + n_a + n_o]
        vo = refs[n_r + n_v + n_a + n_o:]
        i = pl.program_id(0)
        ro, vvals = fn(i * tm, [x[...] for x in r], [x[...] for x in v])
        for ref, val in zip(o, ro):
            ref[...] = val.astype(ref.dtype)
        if n_vo:
            @pl.when(i == 0)
            def _():
                for ref in vo:
                    ref[...] = jnp.zeros_like(ref)
            for ref, val in zip(vo, vvals):
                ref[...] += val

    def colmap(cb):
        return lambda i: (i, cb)

    in_specs = [pl.BlockSpec((tm, w), colmap(cb)) for (_, w, cb) in rows]
    in_specs += [pl.BlockSpec(a.shape, lambda i: (0, 0)) for a in vecs]
    in_specs += [pl.BlockSpec(memory_space=pl.ANY)] * n_a
    out_shape, out_specs, aliases = [], [], {}
    ai = 0
    for k, (w, dt, tot, cb, al) in enumerate(outs):
        out_shape.append(SDS((R, tot if tot else w), dt))
        out_specs.append(pl.BlockSpec((tm, w), colmap(cb)))
        if al is not None:
            aliases[n_r + n_v + ai] = k
            ai += 1
    for (vv, w) in vouts:
        out_shape.append(SDS((vv, w), F32))
        out_specs.append(pl.BlockSpec((vv, w), lambda i: (0, 0)))
    res = pl.pallas_call(
        body, grid=(R // tm,), in_specs=in_specs, out_specs=out_specs, out_shape=out_shape,
        input_output_aliases=aliases, name=name, compiler_params=_cp(("arbitrary",)),
    )(*[a for a, _, _ in rows], *vecs, *alias_in)
    return list(res)


def _vjp_fn(f, n_rows, want_rows, want_vecs, add_last=False):
    def fb(row0, tiles, vecs):
        rows = [t.astype(F32) for t in tiles[:n_rows]]
        outs, vjp = jax.vjp(lambda rs, vs: f(row0, rs, vs), rows, [v.astype(F32) for v in vecs])
        cts = [t.astype(F32) for t in tiles[n_rows:n_rows + len(outs)]]
        g_rows, g_vecs = vjp(cts)
        gr = [g for g, w in zip(g_rows, want_rows) if w]
        if add_last:
            gr[0] = gr[0] + tiles[-1].astype(F32)
        return gr, [g for g, w in zip(g_vecs, want_vecs) if w]
    return fb


def _silu(x):
    return x * jax.nn.sigmoid(x)


def _rms(x, g):
    return x * lax.rsqrt(jnp.mean(x * x, axis=-1, keepdims=True) + EPS) * g


def _is_ctx(row0, tm, lc):
    return (row0 + lax.broadcasted_iota(jnp.int32, (tm, 1), 0)) < lc


ROW_TILES = (768, 512, 384, 256, 128, 64, 32, 16, 8)
COL_TILES = (1024, 768, 512, 384, 256, 128)


def _matmul(name, mode, lhs, rhs, S=1, out_dtype=F32, bias=None, epilogue=None, extra=None, rider=None):
    if mode == "nn":
        M, K = lhs.shape
        Ns = rhs.shape[2]
        t0, t1, t2 = _pick(M, ROW_TILES), _pick(Ns, COL_TILES), _pick(K, (2048, 1024, 512, 256, 128))
        per = Ns // t1
        grid = (M // t0, rhs.shape[0] * per, K // t2)
        lhs_spec = pl.BlockSpec((t0, t2), lambda i, j, k: (i, k))
        rhs_spec = pl.BlockSpec((None, t2, t1), lambda i, j, k: (j // per, k, j % per))
        out_spec = pl.BlockSpec((t0, t1), lambda i, j, k: (i, j))
        out_shape = (M, rhs.shape[0] * Ns)
        contract = ((1,), (0,))
    elif mode == "nt":
        M = lhs.shape[0]
        _, K, Ns = rhs.shape
        t0, t1, t2 = _pick(M, ROW_TILES), _pick(K, (1024, 512, 256, 128)), _pick(Ns, (2048,) + COL_TILES)
        per = Ns // t2
        grid = (M // t0, K // t1, rhs.shape[0] * per)
        lhs_spec = pl.BlockSpec((t0, t2), lambda i, j, k: (i, k))
        rhs_spec = pl.BlockSpec((None, t1, t2), lambda i, j, k: (k // per, j, k % per))
        out_spec = pl.BlockSpec((t0, t1), lambda i, j, k: (i, j))
        out_shape = (M, K)
        contract = ((1,), (1,))
    else:
        M, K = lhs.shape
        Ns = rhs.shape[1] // S
        t0, t1, t2 = _pick(K, (2048, 1024, 512, 256, 128)), _pick(Ns, COL_TILES), _pick(M, ROW_TILES)
        per = Ns // t1
        grid = (K // t0, S * per, M // t2)
        lhs_spec = pl.BlockSpec((t2, t0), lambda i, j, k: (k, i))
        rhs_spec = pl.BlockSpec((t2, t1), lambda i, j, k: (k, j))
        out_spec = pl.BlockSpec((None, t0, t1), lambda i, j, k: (j // per, i, j % per))
        out_shape = (S, K, Ns)
        contract = ((0,), (0,))
    has_bias, has_extra, has_rider = bias is not None, extra is not None, rider is not None
    n_in = 2 + has_bias + has_extra + has_rider
    g0, g1, g2 = grid

    def body(*refs):
        l_ref, r_ref = refs[0], refs[1]
        bias_ref = refs[2] if has_bias else None
        e_ref = refs[2 + has_bias] if has_extra else None
        o_ref = refs[n_in]
        acc_ref = refs[n_in + 1 + has_rider]
        i, j, k = pl.program_id(0), pl.program_id(1), pl.program_id(2)
        if has_rider:
            start, mid, finish = RIDERS[rider[0]](refs[n_in - 1], refs[n_in + 1], *refs[n_in + 3:])
            pl.when((i == 0) & (j == 0) & (k == 0))(start)
            pl.when((i == g0 // 2) & (j == 0) & (k == 0))(mid)

        @pl.when(k == 0)
        def _():
            acc_ref[...] = jnp.zeros_like(acc_ref)

        acc_ref[...] += lax.dot_general(l_ref[...].astype(BF16), r_ref[...].astype(BF16), (contract, ((), ())),
                                        preferred_element_type=F32)

        @pl.when(k == g2 - 1)
        def _():
            acc = acc_ref[...]
            if has_bias:
                acc = acc + bias_ref[...]
            if epilogue is not None:
                acc = epilogue(acc, e_ref[...]) if has_extra else epilogue(acc)
            o_ref[...] = acc.astype(o_ref.dtype)

        if has_rider:
            pl.when((i == g0 - 1) & (j == g1 - 1) & (k == g2 - 1))(finish)

    in_specs, args = [lhs_spec, rhs_spec], [lhs, rhs]
    if has_bias:
        in_specs.append(pl.BlockSpec((1, out_spec.block_shape[-1]), lambda i, j, k: (0, j)))
        args.append(bias)
    if has_extra:
        in_specs.append(out_spec)
        args.append(extra)
    out_specs, out_shapes = [out_spec], [SDS(out_shape, out_dtype)]
    scratch = [pltpu.VMEM(tuple(d for d in out_spec.block_shape if d is not None), F32)]
    if has_rider:
        in_specs.append(pl.BlockSpec(memory_space=pl.ANY))
        args.append(rider[1])
        out_specs.append(pl.BlockSpec(memory_space=pl.ANY))
        out_shapes.append(_rider_out(*rider))
        scratch += _comm_scratch()
    res = pl.pallas_call(
        body, grid=grid, in_specs=in_specs, out_specs=out_specs, out_shape=out_shapes, scratch_shapes=scratch,
        name=name, compiler_params=_cp(("arbitrary", "arbitrary", "arbitrary")),
    )(*args)
    return (res[0], res[1]) if has_rider else res[0]


def _my_pos():
    return lax.axis_index("x"), lax.axis_index("y"), lax.axis_index("c")


def _flip(pos, k):
    x, y, c = pos
    return (1 - x if k & 4 else x, 1 - y if k & 2 else y, 1 - c if k & 1 else c)


def _lin(pos):
    return 4 * pos[0] + 2 * pos[1] + pos[2]


def _all_gather(name, x):
    r, c = x.shape

    def body(x_ref, out_ref, *sems):
        start, mid, finish = _gather_steps(x_ref, out_ref, *sems)
        start()
        mid()
        finish()

    return pl.pallas_call(
        body, out_shape=SDS((N_DEV, r, c), x.dtype),
        in_specs=[pl.BlockSpec(memory_space=pl.ANY)], out_specs=pl.BlockSpec(memory_space=pl.ANY),
        scratch_shapes=_comm_scratch(), name=name, compiler_params=_cp(),
    )(x)


def _comm_scratch():
    return [pltpu.SemaphoreType.DMA((7,)), pltpu.SemaphoreType.DMA((7,)), pltpu.SemaphoreType.DMA(())]


def _gather_steps(x_ref, out_ref, send_sems, recv_sems, local_sem):
    me = _my_pos()
    sibling = _flip(me, 1)
    chips = [2, 4, 6]

    def copy(k, block, to, src=None):
        dst = out_ref.at[_lin(block)]
        return pltpu.make_async_remote_copy(
            src_ref=dst if src is None else src, dst_ref=dst,
            send_sem=send_sems.at[k], recv_sem=recv_sems.at[k], device_id=to, device_id_type=MESH)

    def mine():
        return pltpu.make_async_copy(x_ref, out_ref.at[_lin(me)], local_sem)

    def first():
        return [copy(0, me, sibling, src=x_ref)] + [copy(1 + j, me, _flip(me, m), src=x_ref) for j, m in enumerate(chips)]

    def passed():
        return [copy(4 + j, _flip(me, m), sibling) for j, m in enumerate(chips)]

    def start():
        mine().start()
        for cp in first():
            cp.start()

    def mid():
        fw = passed()
        for j, m in enumerate(chips):
            copy(1 + j, _flip(me, m), me).wait_recv()
            fw[j].start()

    def finish():
        copy(0, sibling, me).wait_recv()
        for j, m in enumerate(chips):
            copy(4 + j, _flip(sibling, m), me).wait_recv()
        for cp in first() + passed():
            cp.wait_send()
        mine().wait()

    return start, mid, finish


def _exchange_steps(p_ref, out_ref, send_sems, recv_sems, local_sem):
    me = _my_pos()
    my_id = _lin(me)

    def mine():
        return pltpu.make_async_copy(p_ref.at[my_id], out_ref.at[my_id], local_sem)

    def sends():
        out = []
        for k in range(1, N_DEV):
            peer = _flip(me, k)
            out.append(pltpu.make_async_remote_copy(
                src_ref=p_ref.at[_lin(peer)], dst_ref=out_ref.at[my_id],
                send_sem=send_sems.at[k - 1], recv_sem=recv_sems.at[k - 1], device_id=peer, device_id_type=MESH))
        return out

    def start():
        mine().start()
        for cp in sends():
            cp.start()

    def mid():
        pass

    def finish():
        for k in range(1, N_DEV):
            peer = _flip(me, k)
            pltpu.make_async_remote_copy(
                src_ref=p_ref.at[my_id], dst_ref=out_ref.at[_lin(peer)],
                send_sem=send_sems.at[k - 1], recv_sem=recv_sems.at[k - 1], device_id=peer, device_id_type=MESH).wait_recv()
        for cp in sends():
            cp.wait_send()
        mine().wait()

    return start, mid, finish


RIDERS = {"gather": _gather_steps, "exchange": _exchange_steps}


def _rider_out(kind, payload):
    return SDS((N_DEV,) + tuple(payload.shape), payload.dtype) if kind == "gather" else SDS(payload.shape, payload.dtype)


def _exchange(name, p3):
    def body(p_ref, out_ref, *sems):
        start, mid, finish = _exchange_steps(p_ref, out_ref, *sems)
        start()
        finish()

    return pl.pallas_call(
        body, out_shape=SDS(p3.shape, p3.dtype),
        in_specs=[pl.BlockSpec(memory_space=pl.ANY)], out_specs=pl.BlockSpec(memory_space=pl.ANY),
        scratch_shapes=_comm_scratch(), name=name, compiler_params=_cp(),
    )(p3)


def _sum_blocks(name, g3):
    n, r, c = g3.shape
    tr = _pick(r, (512, 256, 128, 64, 32, 16, 8))
    tc = _pick(c, (2048, 1024, 512, 256, 128))

    def body(g_ref, o_ref):
        acc = g_ref[0].astype(F32)
        for d in range(1, n):
            acc = acc + g_ref[d].astype(F32)
        o_ref[...] = acc

    return pl.pallas_call(
        body, grid=(r // tr, c // tc),
        in_specs=[pl.BlockSpec((n, tr, tc), lambda i, j: (0, i, j))],
        out_specs=pl.BlockSpec((tr, tc), lambda i, j: (i, j)),
        out_shape=SDS((r, c), F32), name=name, compiler_params=_cp(("parallel", "parallel")),
    )(g3)


def _reduce_scatter(name, p3):
    return _sum_blocks(name + "_sum", _exchange(name + "_xchg", p3))


def _all_reduce(name, v):
    return _sum_blocks(name + "_sum", _all_gather(name + "_ag", v))


def _conv_geometry(K, lc):
    tb = _pick(lc, (256, 128, 64, 32, 16, 8))
    hal = -(-(K // 2) // SUBLANES) * SUBLANES
    return tb, hal


CONV_ROWS = 32


def _conv_layout(seg, tb, hal):
    segb = min(seg, tb)
    return segb, tb // segb, segb + 2 * hal


def _conv_scr_rows(tb, hal, seg_c, seg_l):
    return max(ns * stride for _, ns, stride in (_conv_layout(seg_c, tb, hal), _conv_layout(seg_l, tb, hal)))


def _conv_parts(i, tb, lc, seg_c, seg_l, part):
    if seg_c == seg_l:
        part(seg_c, 0)
        return

    @pl.when(i * tb < lc)
    def _():
        part(seg_c, 0)

    @pl.when(i * tb >= lc)
    def _():
        part(seg_l, lc)


def _conv_fill(scr, prev_ref, cur_ref, next_ref, i, tb, hal, seg, row0):
    segb, ns, stride = _conv_layout(seg, tb, hal)
    cb = cur_ref.shape[1]
    for s in range(ns):
        base = s * stride
        if seg > tb:
            at_start = ((i * tb - row0) & (seg - 1)) == 0
            at_end = ((i * tb + tb - row0) & (seg - 1)) == 0
            scr[pl.ds(base, hal), :] = jnp.where(at_start, 0.0, prev_ref[...])
            scr[pl.ds(base + hal + segb, hal), :] = jnp.where(at_end, 0.0, next_ref[...])
        else:
            scr[pl.ds(base, hal), :] = jnp.zeros((hal, cb), F32)
            scr[pl.ds(base + hal + segb, hal), :] = jnp.zeros((hal, cb), F32)
        scr[pl.ds(base + hal, segb), :] = cur_ref[pl.ds(s * segb, segb), :]


def _conv_tiles(tb, hal, seg):
    segb, ns, stride = _conv_layout(seg, tb, hal)
    rb = min(CONV_ROWS, segb)
    return [(s * stride + hal + r0, s * segb + r0, rb) for s in range(ns) for r0 in range(0, segb, rb)]


def _conv_specs(tb, hal, cb, nrb, col0):
    q = tb // hal
    nh = nrb * q
    return [pl.BlockSpec((hal, cb), lambda j, i: (jnp.maximum(i * q - 1, 0), col0 + j)),
            pl.BlockSpec((tb, cb), lambda j, i: (i, col0 + j)),
            pl.BlockSpec((hal, cb), lambda j, i: (jnp.minimum((i + 1) * q, nh - 1), col0 + j))]


def _conv_fwd(name, x, xcol0, C, w, b, lc, seg_c, seg_l, out=None, ocol0=0):
    R = x.shape[0]
    K = w.shape[0]
    tb, hal = _conv_geometry(K, lc)
    cb = _pick(C, (512, 256, 128))
    nrb = R // tb
    wp = jnp.zeros((-(-K // SUBLANES) * SUBLANES, C), F32).at[:K].set(w)

    def body(*refs):
        prev_ref, cur_ref, next_ref, w_ref, b_ref = refs[:5]
        o_ref, scr = refs[-2], refs[-1]
        i = pl.program_id(1)

        def part(seg, row0):
            _conv_fill(scr, prev_ref, cur_ref, next_ref, i, tb, hal, seg, row0)
            for src, dst, rb in _conv_tiles(tb, hal, seg):
                acc = jnp.broadcast_to(b_ref[...], (rb, cb))
                for k in range(K):
                    acc = acc + scr[pl.ds(src + k - K // 2, rb), :] * w_ref[pl.ds(k, 1), :]
                o_ref[pl.ds(dst, rb), :] = acc

        _conv_parts(i, tb, lc, seg_c, seg_l, part)

    in_specs = _conv_specs(tb, hal, cb, nrb, xcol0 // cb)
    in_specs += [pl.BlockSpec((wp.shape[0], cb), lambda j, i: (0, j)), pl.BlockSpec((1, cb), lambda j, i: (0, j))]
    args = [x, x, x, wp, b.reshape(1, C)]
    aliases = {}
    if out is not None:
        in_specs.append(pl.BlockSpec(memory_space=pl.ANY))
        args.append(out)
        aliases = {5: 0}
    oc0 = ocol0 // cb
    return pl.pallas_call(
        body, grid=(C // cb, nrb), in_specs=in_specs,
        out_specs=pl.BlockSpec((tb, cb), lambda j, i: (i, oc0 + j)),
        out_shape=SDS((R, out.shape[1] if out is not None else C), F32),
        scratch_shapes=[pltpu.VMEM((_conv_scr_rows(tb, hal, seg_c, seg_l), cb), F32)], input_output_aliases=aliases,
        name=name, compiler_params=_cp(("parallel", "arbitrary")),
    )(*args)


def _conv_bwd_w(name, x, xcol0, dy, w_shape, lc, seg_c, seg_l):
    R = x.shape[0]
    K, C = w_shape
    tb, hal = _conv_geometry(K, lc)
    cb = _pick(C, (512, 256, 128))
    nrb = R // tb
    kp = -(-(K + 1) // SUBLANES) * SUBLANES

    def body(prev_ref, cur_ref, next_ref, dy_ref, o_ref, scr, acc8):
        i = pl.program_id(1)

        @pl.when(i == 0)
        def _():
            acc8[...] = jnp.zeros_like(acc8)

        def fold(t):
            return jnp.sum(t.reshape(t.shape[0] // SUBLANES, SUBLANES, cb), axis=0)

        def part(seg, row0):
            _conv_fill(scr, prev_ref, cur_ref, next_ref, i, tb, hal, seg, row0)
            for src, dst, rb in _conv_tiles(tb, hal, seg):
                dy_t = dy_ref[pl.ds(dst, rb), :]
                for k in range(K):
                    acc8[pl.ds(SUBLANES * k, SUBLANES), :] += fold(scr[pl.ds(src + k - K // 2, rb), :] * dy_t)
                acc8[pl.ds(SUBLANES * K, SUBLANES), :] += fold(dy_t)

        _conv_parts(i, tb, lc, seg_c, seg_l, part)

        @pl.when(i == nrb - 1)
        def _():
            o_ref[...] = jnp.zeros_like(o_ref)
            for k in range(K + 1):
                o_ref[pl.ds(k, 1), :] = jnp.sum(acc8[pl.ds(SUBLANES * k, SUBLANES), :], axis=0, keepdims=True)

    in_specs = _conv_specs(tb, hal, cb, nrb, xcol0 // cb)
    in_specs.append(pl.BlockSpec((tb, cb), lambda j, i: (i, j)))
    res = pl.pallas_call(
        body, grid=(C // cb, nrb), in_specs=in_specs,
        out_specs=pl.BlockSpec((kp, cb), lambda j, i: (0, j)),
        out_shape=SDS((kp, C), F32),
        scratch_shapes=[pltpu.VMEM((_conv_scr_rows(tb, hal, seg_c, seg_l), cb), F32),
                        pltpu.VMEM((SUBLANES * (K + 1), cb), F32)],
        name=name, compiler_params=_cp(("parallel", "arbitrary")),
    )(x, x, x, dy)
    return res[:K], res[K]


def _bdot(dims):
    (ca, cb) = dims

    def raw(a, b, dn):
        return lax.dot_general(a.astype(BF16), b.astype(BF16), (dn, ((), ())), preferred_element_type=F32)

    @jax.custom_vjp
    def f(a, b):
        return raw(a, b, ((ca,), (cb,)))

    def fwd(a, b):
        return f(a, b), (a, b)

    def bwd(res, g):
        a, b = res
        if ca == 1:
            da = raw(g, b, ((1,), (1 - cb,)))
        else:
            da = raw(b, g, ((1 - cb,), (1,)))
        if cb == 0:
            db = raw(a, g, ((1 - ca,), (0,)))
        else:
            db = raw(g, a, ((0,), (1 - ca,)))
        return da, db

    f.defvjp(fwd, bwd)
    return f


_dot_nn = _bdot((1, 0))
_dot_nt = _bdot((1, 1))
_dot_tn = _bdot((0, 0))


def _split3(x):
    hi = x.astype(BF16)
    r1 = x - hi.astype(F32)
    mid = r1.astype(BF16)
    lo = (r1 - mid.astype(F32)).astype(BF16)
    return hi, mid, lo


def _tri_raw(tri, x):
    acc = None
    for part in _split3(x):
        t = jnp.dot(tri, part, preferred_element_type=F32)
        acc = t if acc is None else acc + t
    return acc


@jax.custom_vjp
def _tri_mm(tri, x):
    return _tri_raw(tri, x)


def _tri_mm_fwd(tri, x):
    return _tri_raw(tri, x), tri


def _tri_mm_bwd(tri, g):
    return jnp.zeros_like(tri), _tri_raw(tri.T, g)


_tri_mm.defvjp(_tri_mm_fwd, _tri_mm_bwd)


def _softplus(x):
    return jnp.maximum(x, 0.0) + jnp.log(1.0 + jnp.exp(-jnp.abs(x)))


def _dt_chunk(raw, bias, alog, nheads):
    q = raw.shape[0]
    dt = _softplus(raw + bias)
    dta = dt * (-jnp.exp(alog))
    li = lax.broadcasted_iota(jnp.int32, (q, q), 0)
    si = lax.broadcasted_iota(jnp.int32, (q, q), 1)
    lower = (si <= li).astype(BF16)
    upper = (si >= li).astype(BF16)
    col = lax.broadcasted_iota(jnp.int32, raw.shape, 1)
    cum = jnp.where(col < nheads, _tri_mm(lower, dta), _tri_mm(upper, dta))
    return dt.T, cum.T


def _dt_prep(name, proj, col_block, bias, alog, nheads):
    R = proj.shape[0]
    q = CHUNK

    def body(p_ref, b_ref, a_ref, dt_ref, cum_ref):
        dtt, cumt = _dt_chunk(p_ref[...], b_ref[...], a_ref[...], nheads)
        dt_ref[...] = dtt
        cum_ref[...] = cumt

    return pl.pallas_call(
        body, grid=(R // q,),
        in_specs=[pl.BlockSpec((q, LANES), lambda i: (i, col_block)),
                  pl.BlockSpec((1, LANES), lambda i: (0, 0)), pl.BlockSpec((1, LANES), lambda i: (0, 0))],
        out_specs=[pl.BlockSpec((LANES, q), lambda i: (0, i))] * 2,
        out_shape=[SDS((LANES, R), F32)] * 2, name=name, compiler_params=_cp(("parallel",)),
    )(proj, bias, alog)


def _dt_prep_bwd(name, proj, col_block, bias, alog, nheads, ddt, dcum, out, out_col_block):
    R = proj.shape[0]
    q = CHUNK
    wide = 4 * LANES

    def body(p_ref, b_ref, a_ref, g1_ref, g2_ref, out_any, o_ref, db_ref, da_ref):
        i = pl.program_id(0)
        _, vjp = jax.vjp(lambda r, b, a: _dt_chunk(r, b, a, nheads), p_ref[...], b_ref[...], a_ref[...])
        draw, db, da = vjp((g1_ref[...], g2_ref[...]))
        o_ref[...] = jnp.concatenate([draw, jnp.zeros((q, wide - LANES), F32)], axis=1)

        @pl.when(i == 0)
        def _():
            db_ref[...] = jnp.zeros_like(db_ref)
            da_ref[...] = jnp.zeros_like(da_ref)

        db_ref[...] += db
        da_ref[...] += da

    return pl.pallas_call(
        body, grid=(R // q,),
        in_specs=[pl.BlockSpec((q, LANES), lambda i: (i, col_block)),
                  pl.BlockSpec((1, LANES), lambda i: (0, 0)), pl.BlockSpec((1, LANES), lambda i: (0, 0)),
                  pl.BlockSpec((LANES, q), lambda i: (0, i)), pl.BlockSpec((LANES, q), lambda i: (0, i)),
                  pl.BlockSpec(memory_space=pl.ANY)],
        out_specs=[pl.BlockSpec((q, wide), lambda i: (i, out_col_block)),
                   pl.BlockSpec((1, LANES), lambda i: (0, 0)), pl.BlockSpec((1, LANES), lambda i: (0, 0))],
        out_shape=[SDS(out.shape, F32), SDS((1, LANES), F32), SDS((1, LANES), F32)],
        input_output_aliases={5: 0}, name=name, compiler_params=_cp(("arbitrary",)),
    )(proj, bias, alog, ddt, dcum, out)


def _ssd_chunk(rev, xpre, bpre, cpre, dt_rows, a_rows, hin, dvec):
    q, gw = xpre.shape
    hg = len(dt_rows)
    x, bm, cm = _silu(xpre), _silu(bpre), _silu(cpre)
    li = lax.broadcasted_iota(jnp.int32, (q, q), 0)
    si = lax.broadcasted_iota(jnp.int32, (q, q), 1)
    mask = (li <= si) if rev else (li >= si)
    lane = lax.broadcasted_iota(jnp.int32, (q, LANES), 1)
    lane1 = lax.broadcasted_iota(jnp.int32, (1, LANES), 1)
    laneq = lax.broadcasted_iota(jnp.int32, (1, q), 1)
    end = 0 if rev else q - 1
    scores = _dot_nt(cm, bm)
    dt_cols = [jnp.broadcast_to(r, (LANES, q)).T for r in dt_rows]
    a_cols = [jnp.broadcast_to(r, (LANES, q)).T for r in a_rows]
    tot = [jnp.sum(jnp.where(laneq == end, r, 0.0), axis=1, keepdims=True) for r in a_rows]
    pairs = hg // 2

    def expand(cols, n):
        return jnp.concatenate(
            [jnp.where((lane if n == q else lane1) < HEAD_P, cols[2 * p], cols[2 * p + 1]) for p in range(pairs)], axis=1)

    dt_exp = expand(dt_cols, q)
    a_exp = expand(a_cols, q)
    tot_exp = expand([jnp.broadcast_to(t, (1, LANES)) for t in tot], 1)
    xt = x * dt_exp
    ys = []
    for p in range(pairs):
        xp = xt[:, p * LANES:(p + 1) * LANES]
        acc = None
        for t in range(2):
            j = 2 * p + t
            seg = a_cols[j] - a_rows[j]
            m = scores * jnp.exp(jnp.where(mask, seg, -1e30))
            xm = jnp.where((lane < HEAD_P) if t == 0 else (lane >= HEAD_P), xp, 0.0)
            part = _dot_nn(m, xm)
            acc = part if acc is None else acc + part
        ys.append(acc)
    y_diag = jnp.concatenate(ys, axis=1)
    states = _dot_tn(bm, xt * jnp.exp(tot_exp - a_exp))
    hout = jnp.exp(tot_exp) * hin + states
    y_off = _dot_nn(cm, hin) * jnp.exp(a_exp)
    return y_diag + y_off + dvec * x, hout


SSD_GROUPS_PER_STEP = 2


def _chunk_order(rev, ncc, nct):
    if not rev:
        return lambda t: t
    return lambda t: jnp.where(t < ncc, ncc - 1 - t, nct - 1 - (t - ncc))


def _ssd_fwd(name, rev, xs, bs, cs, dtt, cumt, dvec, nheads, lc):
    R, DI = xs.shape
    q, n = CHUNK, N_STATE
    G = N_GROUPS
    gw = DI // G
    hg = gw // HEAD_P
    nct, ncc = R // q, lc // q
    order = _chunk_order(rev, ncc, nct)
    gpb = SSD_GROUPS_PER_STEP

    def body(x_ref, b_ref, c_ref, dt_ref, cum_ref, d_ref, y_ref, st_ref, h_scr):
        gb, t = pl.program_id(0), pl.program_id(1)

        @pl.when(t == 0)
        def _():
            h_scr[...] = jnp.zeros_like(h_scr)

        for gi in range(gpb):
            base = (nheads if rev else 0) + (gb * gpb + gi) * hg
            dt_rows = [dt_ref[pl.ds(base + j, 1), :] for j in range(hg)]
            a_rows = [cum_ref[pl.ds(base + j, 1), :] for j in range(hg)]
            cols, ncols = slice(gi * gw, (gi + 1) * gw), slice(gi * n, (gi + 1) * n)
            hin = h_scr[gi]
            st_ref[gi] = hin
            y, hout = _ssd_chunk(rev, x_ref[:, cols], b_ref[:, ncols], c_ref[:, ncols], dt_rows, a_rows, hin,
                                 d_ref[:, cols])
            y_ref[:, cols] = y
            h_scr[gi] = hout

    return pl.pallas_call(
        body, grid=(G // gpb, nct),
        in_specs=[pl.BlockSpec((q, gpb * gw), lambda g, t: (order(t), g)),
                  pl.BlockSpec((q, gpb * n), lambda g, t: (order(t), g)),
                  pl.BlockSpec((q, gpb * n), lambda g, t: (order(t), g)),
                  pl.BlockSpec((LANES, q), lambda g, t: (0, order(t))),
                  pl.BlockSpec((LANES, q), lambda g, t: (0, order(t))),
                  pl.BlockSpec((1, gpb * gw), lambda g, t: (0, g))],
        out_specs=[pl.BlockSpec((q, gpb * gw), lambda g, t: (order(t), g)),
                   pl.BlockSpec((None, gpb, n, gw), lambda g, t: (order(t), g, 0, 0))],
        out_shape=[SDS((R, DI), F32), SDS((nct, G, n, gw), F32)],
        scratch_shapes=[pltpu.VMEM((gpb, n, gw), F32)],
        name=name, compiler_params=_cp(("parallel", "arbitrary")),
    )(xs, bs, cs, dtt, cumt, dvec)


def _ssd_bwd(name, rev, xs, bs, cs, dtt, cumt, dvec, st, dy, nheads, lc, acc=None):
    R, DI = xs.shape
    q, n = CHUNK, N_STATE
    G = N_GROUPS
    gw = DI // G
    hg = gw // HEAD_P
    hgp = -(-hg // SUBLANES) * SUBLANES
    nct, ncc = R // q, lc // q
    fwd_order = _chunk_order(rev, ncc, nct)
    order = lambda t: fwd_order(nct - 1 - t)
    has_acc = acc is not None
    gpb = SSD_GROUPS_PER_STEP

    def body(*refs):
        x_ref, b_ref, c_ref, dt_ref, cum_ref, d_ref, st_ref, dy_ref = refs[:8]
        k = 8
        acc_refs = refs[k:k + 3] if has_acc else None
        k += 3 if has_acc else 0
        dx_ref, db_ref, dc_ref, ddt_ref, dcum_ref, dd_ref, dh_scr = refs[k:]
        gb, t = pl.program_id(0), pl.program_id(1)

        @pl.when(t == 0)
        def _():
            dh_scr[...] = jnp.zeros_like(dh_scr)
            dd_ref[...] = jnp.zeros_like(dd_ref)

        if hgp > hg:
            ddt_ref[...] = jnp.zeros_like(ddt_ref)
            dcum_ref[...] = jnp.zeros_like(dcum_ref)
        for gi in range(gpb):
            base = (nheads if rev else 0) + (gb * gpb + gi) * hg
            dt_rows = [dt_ref[pl.ds(base + j, 1), :] for j in range(hg)]
            a_rows = [cum_ref[pl.ds(base + j, 1), :] for j in range(hg)]
            cols, ncols = slice(gi * gw, (gi + 1) * gw), slice(gi * n, (gi + 1) * n)
            _, vjp = jax.vjp(functools.partial(_ssd_chunk, rev), x_ref[:, cols], b_ref[:, ncols], c_ref[:, ncols],
                             dt_rows, a_rows, st_ref[gi], d_ref[:, cols])
            dx, db, dc, ddt, da, dhin, dd = vjp((dy_ref[:, cols], dh_scr[gi]))
            if has_acc:
                dx, db, dc = dx + acc_refs[0][:, cols], db + acc_refs[1][:, ncols], dc + acc_refs[2][:, ncols]
            dx_ref[:, cols] = dx
            db_ref[:, ncols] = db
            dc_ref[:, ncols] = dc
            for j in range(hg):
                ddt_ref[gi, pl.ds(j, 1), :] = ddt[j]
                dcum_ref[gi, pl.ds(j, 1), :] = da[j]
            dd_ref[:, cols] += dd
            dh_scr[gi] = dhin

    row_specs = [pl.BlockSpec((q, gpb * gw), lambda g, t: (order(t), g)),
                 pl.BlockSpec((q, gpb * n), lambda g, t: (order(t), g)),
                 pl.BlockSpec((q, gpb * n), lambda g, t: (order(t), g))]
    in_specs = row_specs + [
        pl.BlockSpec((LANES, q), lambda g, t: (0, order(t))),
        pl.BlockSpec((LANES, q), lambda g, t: (0, order(t))),
        pl.BlockSpec((1, gpb * gw), lambda g, t: (0, g)),
        pl.BlockSpec((None, gpb, n, gw), lambda g, t: (order(t), g, 0, 0)),
        pl.BlockSpec((q, gpb * gw), lambda g, t: (order(t), g))]
    args = [xs, bs, cs, dtt, cumt, dvec, st, dy]
    aliases = {}
    if has_acc:
        in_specs += row_specs
        args += list(acc)
        aliases = {8: 0, 9: 1, 10: 2}
    return pl.pallas_call(
        body, grid=(G // gpb, nct), in_specs=in_specs,
        out_specs=row_specs + [pl.BlockSpec((gpb, hgp, q), lambda g, t: (g, 0, order(t))),
                               pl.BlockSpec((gpb, hgp, q), lambda g, t: (g, 0, order(t))),
                               pl.BlockSpec((1, gpb * gw), lambda g, t: (0, g))],
        out_shape=[SDS(xs.shape, F32), SDS(bs.shape, F32), SDS(cs.shape, F32),
                   SDS((G, hgp, R), F32), SDS((G, hgp, R), F32), SDS((1, DI), F32)],
        scratch_shapes=[pltpu.VMEM((gpb, n, gw), F32)], input_output_aliases=aliases,
        name=name, compiler_params=_cp(("parallel", "arbitrary")),
    )(*args)


def _row_tile(width):
    return 256 if width <= 2048 else 128


def _rowwise_op(name, f, n_rows, out_widths, out_dtype=F32):
    def fwd(*args):
        rows, vecs = args[:n_rows], args[n_rows:]
        R = rows[0].shape[0]
        tm = min(R, _row_tile(max([a.shape[1] for a in rows] + list(out_widths))))
        outs = _rowwise(name, lambda r0, rt, vt: (f(r0, rt, vt), []), R, tm,
                        [(a, a.shape[1], 0) for a in rows], list(vecs),
                        [(w, out_dtype, None, 0, None) for w in out_widths])
        return (outs[0] if len(outs) == 1 else tuple(outs)), args

    def bwd(args, ct):
        rows, vecs = args[:n_rows], args[n_rows:]
        cts = [ct] if len(out_widths) == 1 else list(ct)
        R = rows[0].shape[0]
        tm = min(R, _row_tile(max([a.shape[1] for a in rows] + list(out_widths))))
        fb = _vjp_fn(f, n_rows, [True] * n_rows, [True] * len(vecs))
        res = _rowwise(name + "_bwd", fb, R, tm, [(a, a.shape[1], 0) for a in list(rows) + cts], list(vecs),
                       [(a.shape[1], F32, None, 0, None) for a in rows], [v.shape for v in vecs])
        return tuple(res)

    return _cvjp(fwd, bwd)


def _norm_mod_op(name, lc):
    def f(row0, rows, vecs):
        (h,), (g, shc, scc, shl, scl) = rows, vecs
        ctx = _is_ctx(row0, h.shape[0], lc)
        return [_rms(h, g) * (1.0 + jnp.where(ctx, scc, scl)) + jnp.where(ctx, shc, shl)]

    def fwd(h, *vecs):
        R, D = h.shape
        (u,) = _rowwise(name, lambda r0, rt, vt: (f(r0, rt, vt), []), R, min(R, _row_tile(D)),
                        [(h, D, 0)], list(vecs), [(D, BF16, None, 0, None)])
        return (u, h), (h,) + vecs

    def bwd(res, ct):
        h, vecs = res[0], res[1:]
        du, dhp = ct
        R, D = h.shape
        fb = _vjp_fn(f, 1, [True], [True] * 5, add_last=True)
        out = _rowwise(name + "_bwd", fb, R, min(R, _row_tile(D)), [(h, D, 0), (du, D, 0), (dhp, D, 0)], list(vecs),
                       [(D, F32, None, 0, None)], [v.shape for v in vecs])
        return tuple(out)

    return _cvjp(fwd, bwd)


def _post_res_op(name, lc):
    def branch(row0, rows, vecs):
        (y,), (g, gc, gl, bias) = rows, vecs
        return [jnp.where(_is_ctx(row0, y.shape[0], lc), gc, gl) * _rms(y + bias, g)]

    def full(row0, rows, vecs):
        return [rows[0] + branch(row0, rows[1:], vecs)[0]]

    def fwd(h, y, *vecs):
        R, D = h.shape
        (out,) = _rowwise(name, lambda r0, rt, vt: (full(r0, rt, vt), []), R, min(R, _row_tile(D)),
                          [(h, D, 0), (y, D, 0)], list(vecs), [(D, F32, None, 0, None)])
        return out, (y,) + vecs

    def bwd(res, d):
        y, vecs = res[0], res[1:]
        R, D = y.shape
        fb = _vjp_fn(branch, 1, [True], [True] * 4)
        out = _rowwise(name + "_bwd", fb, R, min(R, _row_tile(D)), [(y, D, 0), (d, D, 0)], list(vecs),
                       [(D, F32, None, 0, None)], [v.shape for v in vecs])
        return (d,) + tuple(out)

    return _cvjp(fwd, bwd)


def _payload(w, wd):
    layout, (k, ns), _ = wd
    wb = w.astype(BF16)
    return wb.reshape(k * ns // LANES, LANES) if layout == "flat" else wb


def _as_b3(g, wd):
    layout, (k, ns), npad = wd
    if layout == "col":
        return g
    if layout == "row":
        return g.reshape(1, N_DEV * k, ns)
    full = jnp.transpose(g.reshape(N_DEV, k, ns), (1, 0, 2)).reshape(k, N_DEV * ns)
    return jnp.pad(full, ((0, 0), (0, npad - N_DEV * ns)))[None]


def _as_blocks(dw3, wd):
    layout, (k, ns), _ = wd
    if layout == "col":
        return dw3
    if layout == "row":
        return dw3.reshape(N_DEV, k, ns)
    p = dw3[0, :, :N_DEV * ns].reshape(k, N_DEV, ns)
    return jnp.transpose(p, (1, 0, 2)).reshape(N_DEV, k * ns // LANES, LANES)


def _n_shards(wd):
    return N_DEV if wd[0] == "col" else 1


def _head_op(name, wd):
    def fwd(w):
        return _as_b3(_all_gather(name, _payload(w, wd)), wd), ()

    def bwd(_, db):
        return (_reduce_scatter(name + "_rs", _as_blocks(db, wd)).reshape(wd[1]),)

    return _cvjp(fwd, bwd)


def _chain_op(name, wd_cur, wd_next):
    def fwd(a, b, wn):
        out, g = _matmul(name, "nn", a, b, rider=("gather", _payload(wn, wd_next)))
        return (out, _as_b3(g, wd_next)), (a, b)

    def bwd(res, ct):
        a, b = res
        dout, dbn = ct
        da, recv = _matmul(name + "_da", "nt", dout, b, out_dtype=a.dtype,
                           rider=("exchange", _as_blocks(dbn, wd_next)))
        dw = _matmul(name + "_dw", "tn", a, dout, S=_n_shards(wd_cur), out_dtype=BF16)
        return da, dw, _sum_blocks(name + "_rsum", recv).reshape(wd_next[1])

    return _cvjp(fwd, bwd)


def _mlp_op(name, wd2, wd_next):
    has_next = wd_next is not None

    def fwd(u, b1, w2, *wn):
        act, g2 = _matmul(name + "_up", "nn", u, b1, out_dtype=BF16,
                          epilogue=lambda acc: jnp.square(jnp.maximum(acc, 0.0)), rider=("gather", _payload(w2, wd2)))
        b2 = _as_b3(g2, wd2)
        if has_next:
            f, gn = _matmul(name + "_down", "nn", act, b2, rider=("gather", _payload(wn[0], wd_next)))
            return (f, _as_b3(gn, wd_next)), (u, act, b1, b2)
        return _matmul(name + "_down_last", "nn", act, b2), (u, act, b1, b2)

    def bwd(res, ct):
        u, act, b1, b2 = res
        dact = dict(out_dtype=BF16, extra=act, epilogue=lambda acc, a: acc * (2.0 * jnp.sqrt(a.astype(F32))))
        if has_next:
            df, dbn = ct
            da, recvn = _matmul(name + "_dact", "nt", df, b2, rider=("exchange", _as_blocks(dbn, wd_next)), **dact)
        else:
            df = ct
            da = _matmul(name + "_dact_last", "nt", df, b2, **dact)
        dw2 = _matmul(name + "_dw2", "tn", act, df, S=1, out_dtype=BF16)
        dw1, recv2 = _matmul(name + "_dw1", "tn", u, da, S=N_DEV, out_dtype=BF16,
                             rider=("exchange", _as_blocks(dw2, wd2)))
        du = _matmul(name + "_du", "nt", da, b1, out_dtype=u.dtype)
        out = (du, dw1, _sum_blocks(name + "_rsum2", recv2).reshape(wd2[1]))
        if has_next:
            out += (_sum_blocks(name + "_rsumn", recvn).reshape(wd_next[1]),)
        return out

    return _cvjp(fwd, bwd)


def _conv_op(name, lc, seg_c, seg_l):
    def fwd(v, w, b):
        return _conv_fwd(name, v, 0, v.shape[1], w, b, lc, seg_c, seg_l), (v, w)

    def bwd(res, dy):
        v, w = res
        C = v.shape[1]
        dv = _conv_fwd(name + "_dx", dy, 0, C, w[::-1], jnp.zeros((C,), F32), lc, seg_c, seg_l)
        dw, db = _conv_bwd_w(name + "_dw", v, 0, dy, w.shape, lc, seg_c, seg_l)
        return dv, dw, db

    return _cvjp(fwd, bwd)


def _ssm_core_op(name, cfg):
    DI, GN, H, lc, L, NP = cfg["DI"], cfg["GN"], cfg["H"], cfg["Lc"], cfg["L"], cfg["NP"]
    parts = [(DI, DI), (2 * DI, GN), (2 * DI + GN, GN)]
    dt_col = 2 * DI + 2 * GN

    def gated(row0, rows, vecs):
        (yf, yb, z), (g,) = rows, vecs
        return [_rms((yf + yb) * _silu(z), g)]

    def fwd(proj, cw, cb, alog, bias, d_f, d_b, ng):
        R = proj.shape[0]
        pre, off = [], 0
        for idx, (col, wd) in enumerate(parts):
            pre.append(_conv_fwd(f"{name}_conv{idx}", proj, col, wd, cw[:, off:off + wd], cb[off:off + wd], lc, lc, L))
            off += wd
        dtt, cumt = _dt_prep(name + "_dt", proj, dt_col // LANES, bias, alog, H)
        yf, stf = _ssd_fwd(name + "_scan_f", False, *pre, dtt, cumt, d_f, H, lc)
        yb, stb = _ssd_fwd(name + "_scan_b", True, *pre, dtt, cumt, d_b, H, lc)
        tm = min(R, _row_tile(DI))
        (out,) = _rowwise(name + "_gate", lambda r0, rt, vt: (gated(r0, rt, vt), []), R, tm,
                          [(yf, DI, 0), (yb, DI, 0), (proj, DI, 0)], [ng], [(DI, BF16, None, 0, None)])
        return out, (proj, cw, alog, bias, d_f, d_b, ng, pre, dtt, cumt, yf, yb, stf, stb)

    def bwd(res, dout):
        proj, cw, alog, bias, d_f, d_b, ng, pre, dtt, cumt, yf, yb, stf, stb = res
        R = proj.shape[0]
        tm = min(R, _row_tile(DI))
        fb = _vjp_fn(gated, 3, [True, False, True], [True])
        dy, dproj, dng = _rowwise(name + "_gate_bwd", fb, R, tm,
                                  [(yf, DI, 0), (yb, DI, 0), (proj, DI, 0), (dout, DI, 0)], [ng],
                                  [(DI, F32, None, 0, None), (DI, F32, NP, 0, None)], [ng.shape])
        rf = _ssd_bwd(name + "_scan_f_bwd", False, *pre, dtt, cumt, d_f, stf, dy, H, lc)
        rb = _ssd_bwd(name + "_scan_b_bwd", True, *pre, dtt, cumt, d_b, stb, dy, H, lc, acc=rf[:3])
        hg = DI // N_GROUPS // HEAD_P

        def head_rows(t3):
            return t3[:, :hg].reshape(H, R)

        zpad = jnp.zeros((LANES - 2 * H, R), F32)
        ddt = jnp.concatenate([head_rows(rf[3]), head_rows(rb[3]), zpad], axis=0)
        dcum = jnp.concatenate([head_rows(rf[4]), head_rows(rb[4]), zpad], axis=0)
        dproj, dbias, dalog = _dt_prep_bwd(name + "_dt_bwd", proj, dt_col // LANES, bias, alog, H, ddt, dcum,
                                           dproj, dt_col // (4 * LANES))
        dws, dbs, off = [], [], 0
        for idx, (col, wd) in enumerate(parts):
            dpart = rb[idx]
            dproj = _conv_fwd(f"{name}_conv{idx}_dx", dpart, 0, wd, cw[::-1, off:off + wd], jnp.zeros((wd,), F32),
                              lc, lc, L, out=dproj, ocol0=col)
            dw, db = _conv_bwd_w(f"{name}_conv{idx}_dw", proj, col, dpart, (cw.shape[0], wd), lc, lc, L)
            dws.append(dw)
            dbs.append(db)
            off += wd
        return (dproj, jnp.concatenate(dws, axis=1), jnp.concatenate(dbs), dalog, dbias, rf[5], rb[5], dng)

    return _cvjp(fwd, bwd)


def _mod_op(name):
    def fwd(s8, w, b):
        depth, d, ns = w.shape
        me = _lin(_my_pos())
        rows = N_DEV * SUBLANES
        s_all = _all_gather(name + "_s", s8).reshape(rows, d)
        wb = w.astype(BF16)
        bcols = lax.dynamic_slice_in_dim(b, me * ns, ns, axis=1)
        part = jnp.concatenate([_matmul(name + "_mm", "nn", s_all, wb[i][None], bias=bcols[i][None])
                                for i in range(depth)], axis=1)
        allm = _all_gather(name + "_m", part)
        mine = lax.dynamic_slice_in_dim(allm, me * SUBLANES, SUBLANES, axis=1)
        m = jnp.transpose(mine.reshape(N_DEV, SUBLANES, depth, ns), (2, 1, 0, 3)).reshape(depth, SUBLANES, N_DEV * ns)
        return m, (s_all, wb)

    def bwd(res, dm):
        s_all, wb = res
        depth, d, ns = wb.shape
        me = _lin(_my_pos())
        rows = N_DEV * SUBLANES
        fac = jnp.transpose(dm, (1, 0, 2)).reshape(SUBLANES, depth * N_DEV * ns)
        allf = _all_gather(name + "_fac", fac).reshape(rows, depth, N_DEV, ns)
        mine = lax.dynamic_slice_in_dim(allf, me, 1, axis=2)[:, :, 0]
        dws = [_matmul(name + "_dw", "tn", s_all, mine[:, i], S=1, out_dtype=F32)[0] for i in range(depth)]
        ds_part = sum(_matmul(name + "_ds", "nt", mine[:, i], wb[i][None]) for i in range(depth))
        ds_all = _all_reduce(name + "_dsr", ds_part)
        ds = lax.dynamic_slice_in_dim(ds_all, me * SUBLANES, SUBLANES, axis=0)
        db = _sum_blocks(name + "_db", allf.reshape(rows, depth * N_DEV * ns // LANES, LANES))
        return ds, jnp.stack(dws), db.reshape(depth, N_DEV * ns)

    return _cvjp(fwd, bwd)


def _loss_grad(name, h, target, lc):
    R, D = h.shape
    tm = min(lc, 256)
    off = lc // tm

    def body(h_ref, t_ref, g_ref, s_ref):
        i = pl.program_id(0)

        @pl.when(i == 0)
        def _():
            s_ref[...] = jnp.zeros_like(s_ref)

        err = jnp.where(i >= off, h_ref[...] - t_ref[...], 0.0)
        g_ref[...] = err * (1.0 / D)
        s_ref[...] += jnp.sum(err * err, axis=0, keepdims=True)

    return pl.pallas_call(
        body, grid=(R // tm,),
        in_specs=[pl.BlockSpec((tm, D), lambda i: (i, 0)),
                  pl.BlockSpec((tm, D), lambda i: (jnp.maximum(i - off, 0), 0))],
        out_specs=[pl.BlockSpec((tm, D), lambda i: (i, 0)), pl.BlockSpec((1, D), lambda i: (0, 0))],
        out_shape=[SDS((R, D), F32), SDS((1, D), F32)], name=name, compiler_params=_cp(("arbitrary",)),
    )(h, target)


def _adam(name, w, g, m, v):
    shape = w.shape
    c = shape[-1]
    r = w.size // c
    tr = _pick(r, (256, 128, 64, 32, 16, 8))
    tc = _pick(c, (2048, 1536, 1024, 768, 512, 256, 128))
    c1 = 1.0 / (1.0 - ADAM_B1 ** ADAM_STEP)
    c2 = 1.0 / (1.0 - ADAM_B2 ** ADAM_STEP)

    def body(w_ref, g_ref, m_ref, v_ref, d_ref, mo_ref, vo_ref):
        gg = g_ref[...]
        mn = ADAM_B1 * m_ref[...] + (1.0 - ADAM_B1) * gg
        vn = ADAM_B2 * v_ref[...] + (1.0 - ADAM_B2) * jnp.square(gg)
        d_ref[...] = -ADAM_LR * ((mn * c1) / (jnp.sqrt(vn * c2) + ADAM_EPS) + ADAM_WD * w_ref[...])
        mo_ref[...] = mn
        vo_ref[...] = vn

    spec = pl.BlockSpec((tr, tc), lambda i, j: (i, j))
    res = pl.pallas_call(
        body, grid=(r // tr, c // tc), in_specs=[spec] * 4, out_specs=[spec] * 3,
        out_shape=[SDS((r, c), F32)] * 3, name=name, compiler_params=_cp(("parallel", "parallel")),
    )(*[a.reshape(r, c) for a in (w, g, m, v)])
    return [a.reshape(shape) for a in res]


WEIGHTS = ["c_ctx", "mod_w", "mod_b", "pre_mix_g", "post_mix_g", "pre_mlp_g", "post_mlp_g", "mlp_w1", "mlp_w2",
           "ssm_in_w", "ssm_conv_w", "ssm_conv_b", "ssm_a_log_f", "ssm_dt_bias_f", "ssm_d_f", "ssm_a_log_b",
           "ssm_dt_bias_b", "ssm_d_b", "ssm_norm_g", "ssm_out_w", "conf_pw1_w", "conf_pw1_b", "conf_dw_w",
           "conf_dw_b", "conf_ln_g", "conf_ln_b", "conf_pw2_w", "conf_pw2_b"]
BIG_SHARDED = ["mod_w", "mlp_w1", "mlp_w2", "ssm_in_w", "ssm_out_w", "conf_pw1_w", "conf_pw2_w"]
SMALL_SHARDED = {"ssm_conv_w": 2, "conf_pw1_b": 1, "conf_dw_w": 2, "conf_dw_b": 1, "conf_ln_g": 1, "conf_ln_b": 1,
                 "conf_pw2_b": 1}


def _pack(arrs):
    flat = jnp.concatenate([a.reshape(-1).astype(F32) for a in arrs])
    n = flat.shape[0]
    tile = LANES * SUBLANES
    npad = -(-n // tile) * tile
    return jnp.pad(flat, (0, npad - n)).reshape(npad // LANES, LANES)


def _unpack(flat, shapes):
    out, off = [], 0
    for s in shapes:
        n = 1
        for d in s:
            n *= d
        out.append(flat[off:off + n].reshape(s))
        off += n
    return out


def _to_col(h, lc, rows_g):
    d = h.shape[1]
    lat = h[lc:].reshape(rows_g, GRID_W, d).swapaxes(0, 1).reshape(-1, d)
    return jnp.concatenate([h[:lc], lat], axis=0)


def _from_col(h, lc, rows_g):
    d = h.shape[1]
    lat = h[lc:].reshape(GRID_W, rows_g, d).swapaxes(0, 1).reshape(-1, d)
    return jnp.concatenate([h[:lc], lat], axis=0)


def _step(a):
    x, ctx, target = a["x"][0], a["ctx"][0], a["loss_target"][0]
    L, D = x.shape
    lc = ctx.shape[0]
    depth = a["mod_w"].shape[0]
    DI = 2 * D
    H = DI // HEAD_P
    GN = N_GROUPS * N_STATE
    cfg = dict(DI=DI, GN=GN, H=H, Lc=lc, L=L, NP=2 * DI + 2 * GN + 4 * LANES)
    rows_g = L // GRID_W
    me = _lin(_my_pos())

    names = list(SMALL_SHARDED)
    gathered = _all_gather("small_w_ag", _pack([a[n] for n in names])).reshape(N_DEV, -1)
    full_small, off = {}, 0
    for n in names:
        shp, ax = a[n].shape, SMALL_SHARDED[n]
        seg = gathered[:, off:off + a[n].size].reshape((N_DEV,) + shp)
        off += a[n].size
        full_small[n] = jnp.moveaxis(seg, 0, ax).reshape(shp[:ax] + (N_DEV * shp[ax],) + shp[ax + 1:])

    params = {n: (full_small[n] if n in SMALL_SHARDED else a[n]) for n in WEIGHTS}
    params["x"] = x

    def trunk(p):
        raw8 = jnp.concatenate([a["c"], p["c_ctx"][None], jnp.zeros((SUBLANES - 2, D), F32)], axis=0)
        s8 = _rowwise_op("silu_c", lambda r0, rows, vecs: [_silu(rows[0])], 1, [D])(raw8)
        mods = _mod_op("mod")(s8, p["mod_w"], p["mod_b"])
        zero_bias = jnp.zeros((1, D), F32)
        h = jnp.concatenate([ctx, p["x"]], axis=0)

        seq = []
        for i in range(depth):
            j = i // 2
            mixer = [("in", "ssm_in_w", "flat"), ("out", "ssm_out_w", "row")] if i % 2 == 0 else \
                    [("pw1", "conf_pw1_w", "col"), ("pw2", "conf_pw2_w", "row")]
            for role, n, layout in mixer:
                seq.append((role, p[n][j], (layout, p[n][j].shape, cfg["NP"])))
            seq.append(("w1", p["mlp_w1"][i], ("col", p["mlp_w1"][i].shape, 0)))
            seq.append(("w2", p["mlp_w2"][i], ("row", p["mlp_w2"][i].shape, 0)))
        state = {"t": 0, "b": _head_op("head_ag", seq[0][2])(seq[0][1])}

        def mm(a):
            t = state["t"]
            out, state["b"] = _chain_op(seq[t][0], seq[t][2], seq[t + 1][2])(a, state["b"], seq[t + 1][1])
            state["t"] = t + 1
            return out

        def mlp(u2):
            t = state["t"]
            if t + 2 < len(seq):
                f, state["b"] = _mlp_op("mlp_" + seq[t + 2][0], seq[t + 1][2], seq[t + 2][2])(
                    u2, state["b"], seq[t + 1][1], seq[t + 2][1])
            else:
                f = _mlp_op("mlp", seq[t + 1][2], None)(u2, state["b"], seq[t + 1][1])
            state["t"] = t + 2
            return f

        for i in range(depth):
            kind, j = i % 2, i // 2
            col_major = (j % 2) == 1
            if i == 2:
                h = _to_col(h, lc, rows_g)
            sh1, sc1, g1, sh2, sc2, g2 = jnp.split(mods[i], 6, axis=1)
            u, hp = _norm_mod_op("pre_norm", lc)(h, p["pre_mix_g"][i][None], sh1[1:2], sc1[1:2], sh1[0:1], sc1[0:1])
            if kind == 0:
                proj = mm(u)
                pad = jnp.zeros((LANES - 2 * H,), F32)
                alog = jnp.concatenate([p["ssm_a_log_f"][j], p["ssm_a_log_b"][j], pad])[None]
                bias = jnp.concatenate([p["ssm_dt_bias_f"][j], p["ssm_dt_bias_b"][j], pad])[None]
                y = _ssm_core_op("ssm", cfg)(
                    proj, p["ssm_conv_w"][j], p["ssm_conv_b"][j], alog, bias,
                    jnp.repeat(p["ssm_d_f"][j], HEAD_P)[None], jnp.repeat(p["ssm_d_b"][j], HEAD_P)[None],
                    p["ssm_norm_g"][j][None])
                o = mm(y)
                mix_bias = zero_bias
            else:
                pre = mm(u)
                glu = _rowwise_op("glu", lambda r0, rows, vecs: [
                    (rows[0] + vecs[0])[:, :D] * jax.nn.sigmoid((rows[0] + vecs[0])[:, D:])], 1, [D])
                v = glu(pre, p["conf_pw1_b"][j][None])
                seg = rows_g if col_major else GRID_W
                cv = _conv_op(f"dw{seg}", lc, lc, seg)(v, p["conf_dw_w"][j], p["conf_dw_b"][j])

                def ln_swish(r0, rows, vecs):
                    xc = rows[0] - jnp.mean(rows[0], axis=-1, keepdims=True)
                    yn = xc * lax.rsqrt(jnp.mean(xc * xc, axis=-1, keepdims=True) + EPS) * vecs[0] + vecs[1]
                    return [_silu(yn)]

                w = _rowwise_op("ln_swish", ln_swish, 1, [D], BF16)(cv, p["conf_ln_g"][j][None], p["conf_ln_b"][j][None])
                o = mm(w)
                mix_bias = p["conf_pw2_b"][j][None]
            h = _post_res_op("post_norm", lc)(hp, o, p["post_mix_g"][i][None], g1[1:2], g1[0:1], mix_bias)
            u2, hp = _norm_mod_op("pre_norm", lc)(h, p["pre_mlp_g"][i][None], sh2[1:2], sc2[1:2], sh2[0:1], sc2[0:1])
            f = mlp(u2)
            h = _post_res_op("post_norm", lc)(hp, f, p["post_mlp_g"][i][None], g2[1:2], g2[0:1], zero_bias)
        return _from_col(h, lc, rows_g) if depth > 2 else h

    h_out, vjp = jax.vjp(trunk, params)
    dh, sq = _loss_grad("loss", h_out, target, lc)
    (grads,) = vjp(dh)
    loss = lax.psum(jnp.sum(sq) * (0.5 / D), ("x", "y", "c"))

    part_names = [n for n in WEIGHTS if n not in BIG_SHARDED and n != "mod_b"]
    red = _all_reduce("small_g", _pack([grads[n] for n in part_names])).reshape(-1)
    reduced = dict(zip(part_names, _unpack(red, [grads[n].shape for n in part_names])))
    final = {}
    for n in WEIGHTS:
        if n in BIG_SHARDED or n == "mod_b":
            final[n] = grads[n]
        elif n in SMALL_SHARDED:
            ax = SMALL_SHARDED[n]
            final[n] = lax.dynamic_slice_in_dim(reduced[n], me * a[n].shape[ax], a[n].shape[ax], axis=ax)
        else:
            final[n] = reduced[n]

    deltas, new_m, new_v = [], [], []
    for n in WEIGHTS:
        d_, m_, v_ = _adam("adam_" + n, a[n], final[n], a["m_" + n], a["v_" + n])
        deltas.append(d_)
        new_m.append(m_)
        new_v.append(v_)
    return (loss, grads["x"][None], *[final[n] for n in WEIGHTS], *deltas, *new_m, *new_v)


def kernel(x, c, ctx, c_ctx, mod_w, mod_b, pre_mix_g, post_mix_g, pre_mlp_g, post_mlp_g, mlp_w1, mlp_w2, ssm_in_w, ssm_conv_w, ssm_conv_b, ssm_a_log_f, ssm_dt_bias_f, ssm_d_f, ssm_a_log_b, ssm_dt_bias_b, ssm_d_b, ssm_norm_g, ssm_out_w, conf_pw1_w, conf_pw1_b, conf_dw_w, conf_dw_b, conf_ln_g, conf_ln_b, conf_pw2_w, conf_pw2_b, loss_target, m_c_ctx, m_mod_w, m_mod_b, m_pre_mix_g, m_post_mix_g, m_pre_mlp_g, m_post_mlp_g, m_mlp_w1, m_mlp_w2, m_ssm_in_w, m_ssm_conv_w, m_ssm_conv_b, m_ssm_a_log_f, m_ssm_dt_bias_f, m_ssm_d_f, m_ssm_a_log_b, m_ssm_dt_bias_b, m_ssm_d_b, m_ssm_norm_g, m_ssm_out_w, m_conf_pw1_w, m_conf_pw1_b, m_conf_dw_w, m_conf_dw_b, m_conf_ln_g, m_conf_ln_b, m_conf_pw2_w, m_conf_pw2_b, v_c_ctx, v_mod_w, v_mod_b, v_pre_mix_g, v_post_mix_g, v_pre_mlp_g, v_post_mlp_g, v_mlp_w1, v_mlp_w2, v_ssm_in_w, v_ssm_conv_w, v_ssm_conv_b, v_ssm_a_log_f, v_ssm_dt_bias_f, v_ssm_d_f, v_ssm_a_log_b, v_ssm_dt_bias_b, v_ssm_d_b, v_ssm_norm_g, v_ssm_out_w, v_conf_pw1_w, v_conf_pw1_b, v_conf_dw_w, v_conf_dw_b, v_conf_ln_g, v_conf_ln_b, v_conf_pw2_w, v_conf_pw2_b):
    return _step(dict(locals()))
```

```python
import functools

import jax
import jax.numpy as jnp
from jax import lax
from jax.experimental import pallas as pl
from jax.experimental.pallas import tpu as pltpu

F32, BF16 = jnp.float32, jnp.bfloat16
SDS = jax.ShapeDtypeStruct

GRID_W = 64
HEAD_P = 64
N_GROUPS = 8
N_STATE = 128
SSM_K = 5
CHUNK = 128
CONF_K = 31
EPS = 1e-6
ADAM_LR, ADAM_B1, ADAM_B2, ADAM_EPS, ADAM_WD, ADAM_STEP = 0.001, 0.9, 0.999, 1e-08, 0.01, 10

LANES = 128
SUBLANES = 8
N_DEV = 8
MESH = pl.DeviceIdType.MESH
VMEM_LIMIT = 56 * 2**20


def _cp(sem=None, vmem=VMEM_LIMIT):
    if sem is None:
        return pltpu.CompilerParams(vmem_limit_bytes=vmem)
    return pltpu.CompilerParams(dimension_semantics=sem, vmem_limit_bytes=vmem)


def _pick(dim, cands):
    for c in cands:
        if c <= dim and dim % c == 0:
            return c
    return dim


def _cvjp(fwd, bwd):
    @jax.custom_vjp
    def op(*args):
        return fwd(*args)[0]
    op.defvjp(fwd, bwd)
    return op


def _rowwise(name, fn, R, tm, rows, vecs, outs, vouts=()):
    n_r, n_v, n_o, n_vo = len(rows), len(vecs), len(outs), len(vouts)
    alias_in = [o[4] for o in outs if o[4] is not None]
    n_a = len(alias_in)

    def body(*refs):
        r = refs[:n_r]
        v = refs[n_r:n_r + n_v]
        o = refs[n_r + n_v + n_a:n_r + n_v + n_a + n_o]
        vo = refs[n_r + n_v + n_a + n_o:]
        i = pl.program_id(0)
        ro, vvals = fn(i * tm, [x[...] for x in r], [x[...] for x in v])
        for ref, val in zip(o, ro):
            ref[...] = val.astype(ref.dtype)
        if n_vo:
            @pl.when(i == 0)
            def _():
                for ref in vo:
                    ref[...] = jnp.zeros_like(ref)
            for ref, val in zip(vo, vvals):
                ref[...] += val

    def colmap(cb):
        return lambda i: (i, cb)

    in_specs = [pl.BlockSpec((tm, w), colmap(cb)) for (_, w, cb) in rows]
    in_specs += [pl.BlockSpec(a.shape, lambda i: (0, 0)) for a in vecs]
    in_specs += [pl.BlockSpec(memory_space=pl.ANY)] * n_a
    out_shape, out_specs, aliases = [], [], {}
    ai = 0
    for k, (w, dt, tot, cb, al) in enumerate(outs):
        out_shape.append(SDS((R, tot if tot else w), dt))
        out_specs.append(pl.BlockSpec((tm, w), colmap(cb)))
        if al is not None:
            aliases[n_r + n_v + ai] = k
            ai += 1
    for (vv, w) in vouts:
        out_shape.append(SDS((vv, w), F32))
        out_specs.append(pl.BlockSpec((vv, w), lambda i: (0, 0)))
    res = pl.pallas_call(
        body, grid=(R // tm,), in_specs=in_specs, out_specs=out_specs, out_shape=out_shape,
        input_output_aliases=aliases, name=name, compiler_params=_cp(("arbitrary",)),
    )(*[a for a, _, _ in rows], *vecs, *alias_in)
    return list(res)


def _vjp_fn(f, n_rows, want_rows, want_vecs, add_last=False):
    def fb(row0, tiles, vecs):
        rows = [t.astype(F32) for t in tiles[:n_rows]]
        outs, vjp = jax.vjp(lambda rs, vs: f(row0, rs, vs), rows, [v.astype(F32) for v in vecs])
        cts = [t.astype(F32) for t in tiles[n_rows:n_rows + len(outs)]]
        g_rows, g_vecs = vjp(cts)
        gr = [g for g, w in zip(g_rows, want_rows) if w]
        if add_last:
            gr[0] = gr[0] + tiles[-1].astype(F32)
        return gr, [g for g, w in zip(g_vecs, want_vecs) if w]
    return fb


def _silu(x):
    return x * jax.nn.sigmoid(x)


def _rms(x, g):
    return x * lax.rsqrt(jnp.mean(x * x, axis=-1, keepdims=True) + EPS) * g


def _is_ctx(row0, tm, lc):
    return (row0 + lax.broadcasted_iota(jnp.int32, (tm, 1), 0)) < lc


ROW_TILES = (768, 512, 384, 256, 128, 64, 32, 16, 8)
COL_TILES = (1024, 768, 512, 384, 256, 128)
NT_VMEM_BUDGET = 44 * 2**20


def _matmul(name, mode, lhs, rhs, S=1, out_dtype=F32, bias=None, epilogue=None, extra=None, rider=None):
    if mode == "nn":
        M, K = lhs.shape
        Ns = rhs.shape[2]
        t0, t1, t2 = _pick(M, ROW_TILES), _pick(Ns, COL_TILES), _pick(K, (2048, 1024, 512, 256, 128))
        per = Ns // t1
        grid = (M // t0, rhs.shape[0] * per, K // t2)
        lhs_spec = pl.BlockSpec((t0, t2), lambda i, j, k: (i, k))
        rhs_spec = pl.BlockSpec((None, t2, t1), lambda i, j, k: (j // per, k, j % per))
        out_spec = pl.BlockSpec((t0, t1), lambda i, j, k: (i, j))
        out_shape = (M, rhs.shape[0] * Ns)
        contract = ((1,), (0,))
    elif mode == "nt":
        M = lhs.shape[0]
        _, K, Ns = rhs.shape
        t0, t2 = _pick(M, ROW_TILES), _pick(Ns, (2048,) + COL_TILES)

        def vmem_bytes(t1):
            per_out = 2 * jnp.dtype(out_dtype).itemsize + 4 + (2 * extra.dtype.itemsize if extra is not None else 0)
            return 2 * (t0 * t2 * lhs.dtype.itemsize + t1 * t2 * rhs.dtype.itemsize) + t0 * t1 * per_out

        t1 = next(c for c in (2048, 1024, 512, 256, 128) if K % c == 0 and (c <= 1024 or vmem_bytes(c) <= NT_VMEM_BUDGET))
        per = Ns // t2
        grid = (M // t0, K // t1, rhs.shape[0] * per)
        lhs_spec = pl.BlockSpec((t0, t2), lambda i, j, k: (i, k))
        rhs_spec = pl.BlockSpec((None, t1, t2), lambda i, j, k: (k // per, j, k % per))
        out_spec = pl.BlockSpec((t0, t1), lambda i, j, k: (i, j))
        out_shape = (M, K)
        contract = ((1,), (1,))
    else:
        M, K = lhs.shape
        Ns = rhs.shape[1] // S
        t0, t1, t2 = _pick(K, (2048, 1024, 512, 256, 128)), _pick(Ns, COL_TILES), _pick(M, ROW_TILES)
        per = Ns // t1
        grid = (K // t0, S * per, M // t2)
        lhs_spec = pl.BlockSpec((t2, t0), lambda i, j, k: (k, i))
        rhs_spec = pl.BlockSpec((t2, t1), lambda i, j, k: (k, j))
        out_spec = pl.BlockSpec((None, t0, t1), lambda i, j, k: (j // per, i, j % per))
        out_shape = (S, K, Ns)
        contract = ((0,), (0,))
    has_bias, has_extra, has_rider = bias is not None, extra is not None, rider is not None
    n_in = 2 + has_bias + has_extra + has_rider
    g0, g1, g2 = grid

    def body(*refs):
        l_ref, r_ref = refs[0], refs[1]
        bias_ref = refs[2] if has_bias else None
        e_ref = refs[2 + has_bias] if has_extra else None
        o_ref = refs[n_in]
        acc_ref = refs[n_in + 1 + has_rider]
        i, j, k = pl.program_id(0), pl.program_id(1), pl.program_id(2)
        if has_rider:
            start, mid, finish = RIDERS[rider[0]](refs[n_in - 1], refs[n_in + 1], *refs[n_in + 3:])
            pl.when((i == 0) & (j == 0) & (k == 0))(start)
            pl.when((i == g0 // 2) & (j == 0) & (k == 0))(mid)

        @pl.when(k == 0)
        def _():
            acc_ref[...] = jnp.zeros_like(acc_ref)

        acc_ref[...] += lax.dot_general(l_ref[...].astype(BF16), r_ref[...].astype(BF16), (contract, ((), ())),
                                        preferred_element_type=F32)

        @pl.when(k == g2 - 1)
        def _():
            acc = acc_ref[...]
            if has_bias:
                acc = acc + bias_ref[...]
            if epilogue is not None:
                acc = epilogue(acc, e_ref[...]) if has_extra else epilogue(acc)
            o_ref[...] = acc.astype(o_ref.dtype)

        if has_rider:
            pl.when((i == g0 - 1) & (j == g1 - 1) & (k == g2 - 1))(finish)

    in_specs, args = [lhs_spec, rhs_spec], [lhs, rhs]
    if has_bias:
        in_specs.append(pl.BlockSpec((1, out_spec.block_shape[-1]), lambda i, j, k: (0, j)))
        args.append(bias)
    if has_extra:
        in_specs.append(out_spec)
        args.append(extra)
    out_specs, out_shapes = [out_spec], [SDS(out_shape, out_dtype)]
    scratch = [pltpu.VMEM(tuple(d for d in out_spec.block_shape if d is not None), F32)]
    if has_rider:
        in_specs.append(pl.BlockSpec(memory_space=pl.ANY))
        args.append(rider[1])
        out_specs.append(pl.BlockSpec(memory_space=pl.ANY))
        out_shapes.append(_rider_out(*rider))
        scratch += _comm_scratch()
    res = pl.pallas_call(
        body, grid=grid, in_specs=in_specs, out_specs=out_specs, out_shape=out_shapes, scratch_shapes=scratch,
        name=name, compiler_params=_cp(("arbitrary", "arbitrary", "arbitrary")),
    )(*args)
    return (res[0], res[1]) if has_rider else res[0]


def _my_pos():
    return lax.axis_index("x"), lax.axis_index("y"), lax.axis_index("c")


def _flip(pos, k):
    x, y, c = pos
    return (1 - x if k & 4 else x, 1 - y if k & 2 else y, 1 - c if k & 1 else c)


def _lin(pos):
    return 4 * pos[0] + 2 * pos[1] + pos[2]


def _all_gather(name, x):
    r, c = x.shape

    def body(x_ref, out_ref, *sems):
        start, mid, finish = _gather_steps(x_ref, out_ref, *sems)
        start()
        mid()
        finish()

    return pl.pallas_call(
        body, out_shape=SDS((N_DEV, r, c), x.dtype),
        in_specs=[pl.BlockSpec(memory_space=pl.ANY)], out_specs=pl.BlockSpec(memory_space=pl.ANY),
        scratch_shapes=_comm_scratch(), name=name, compiler_params=_cp(),
    )(x)


def _comm_scratch():
    return [pltpu.SemaphoreType.DMA((7,)), pltpu.SemaphoreType.DMA((7,)), pltpu.SemaphoreType.DMA(())]


def _gather_steps(x_ref, out_ref, send_sems, recv_sems, local_sem):
    me = _my_pos()
    sibling = _flip(me, 1)
    chips = [2, 4, 6]

    def copy(k, block, to, src=None):
        dst = out_ref.at[_lin(block)]
        return pltpu.make_async_remote_copy(
            src_ref=dst if src is None else src, dst_ref=dst,
            send_sem=send_sems.at[k], recv_sem=recv_sems.at[k], device_id=to, device_id_type=MESH)

    def mine():
        return pltpu.make_async_copy(x_ref, out_ref.at[_lin(me)], local_sem)

    def first():
        return [copy(0, me, sibling, src=x_ref)] + [copy(1 + j, me, _flip(me, m), src=x_ref) for j, m in enumerate(chips)]

    def passed():
        return [copy(4 + j, _flip(me, m), sibling) for j, m in enumerate(chips)]

    def start():
        mine().start()
        for cp in first():
            cp.start()

    def mid():
        fw = passed()
        for j, m in enumerate(chips):
            copy(1 + j, _flip(me, m), me).wait_recv()
            fw[j].start()

    def finish():
        copy(0, sibling, me).wait_recv()
        for j, m in enumerate(chips):
            copy(4 + j, _flip(sibling, m), me).wait_recv()
        for cp in first() + passed():
            cp.wait_send()
        mine().wait()

    return start, mid, finish


def _exchange_steps(p_ref, out_ref, send_sems, recv_sems, local_sem):
    me = _my_pos()
    my_id = _lin(me)

    def mine():
        return pltpu.make_async_copy(p_ref.at[my_id], out_ref.at[my_id], local_sem)

    def sends():
        out = []
        for k in range(1, N_DEV):
            peer = _flip(me, k)
            out.append(pltpu.make_async_remote_copy(
                src_ref=p_ref.at[_lin(peer)], dst_ref=out_ref.at[my_id],
                send_sem=send_sems.at[k - 1], recv_sem=recv_sems.at[k - 1], device_id=peer, device_id_type=MESH))
        return out

    def start():
        mine().start()
        for cp in sends():
            cp.start()

    def mid():
        pass

    def finish():
        for k in range(1, N_DEV):
            peer = _flip(me, k)
            pltpu.make_async_remote_copy(
                src_ref=p_ref.at[my_id], dst_ref=out_ref.at[_lin(peer)],
                send_sem=send_sems.at[k - 1], recv_sem=recv_sems.at[k - 1], device_id=peer, device_id_type=MESH).wait_recv()
        for cp in sends():
            cp.wait_send()
        mine().wait()

    return start, mid, finish


RIDERS = {"gather": _gather_steps, "exchange": _exchange_steps}


def _rider_out(kind, payload):
    return SDS((N_DEV,) + tuple(payload.shape), payload.dtype) if kind == "gather" else SDS(payload.shape, payload.dtype)


def _exchange(name, p3):
    def body(p_ref, out_ref, *sems):
        start, mid, finish = _exchange_steps(p_ref, out_ref, *sems)
        start()
        finish()

    return pl.pallas_call(
        body, out_shape=SDS(p3.shape, p3.dtype),
        in_specs=[pl.BlockSpec(memory_space=pl.ANY)], out_specs=pl.BlockSpec(memory_space=pl.ANY),
        scratch_shapes=_comm_scratch(), name=name, compiler_params=_cp(),
    )(p3)


def _sum_blocks(name, g3):
    n, r, c = g3.shape
    tr = _pick(r, (512, 256, 128, 64, 32, 16, 8))
    tc = _pick(c, (2048, 1024, 512, 256, 128))

    def body(g_ref, o_ref):
        acc = g_ref[0].astype(F32)
        for d in range(1, n):
            acc = acc + g_ref[d].astype(F32)
        o_ref[...] = acc

    return pl.pallas_call(
        body, grid=(r // tr, c // tc),
        in_specs=[pl.BlockSpec((n, tr, tc), lambda i, j: (0, i, j))],
        out_specs=pl.BlockSpec((tr, tc), lambda i, j: (i, j)),
        out_shape=SDS((r, c), F32), name=name, compiler_params=_cp(("parallel", "parallel")),
    )(g3)


def _reduce_scatter(name, p3):
    return _sum_blocks(name + "_sum", _exchange(name + "_xchg", p3))


def _all_reduce(name, v):
    return _sum_blocks(name + "_sum", _all_gather(name + "_ag", v))


def _conv_geometry(K, lc):
    tb = _pick(lc, (256, 128, 64, 32, 16, 8))
    hal = -(-(K // 2) // SUBLANES) * SUBLANES
    return tb, hal


CONV_ROWS = 32


def _conv_layout(seg, tb, hal):
    segb = min(seg, tb)
    return segb, tb // segb, segb + 2 * hal


def _conv_scr_rows(tb, hal, seg_c, seg_l):
    return max(ns * stride for _, ns, stride in (_conv_layout(seg_c, tb, hal), _conv_layout(seg_l, tb, hal)))


def _conv_parts(i, tb, lc, seg_c, seg_l, part):
    if seg_c == seg_l:
        part(seg_c, 0)
        return

    @pl.when(i * tb < lc)
    def _():
        part(seg_c, 0)

    @pl.when(i * tb >= lc)
    def _():
        part(seg_l, lc)


def _conv_fill(scr, prev_ref, cur_ref, next_ref, i, tb, hal, seg, row0):
    segb, ns, stride = _conv_layout(seg, tb, hal)
    cb = cur_ref.shape[1]
    for s in range(ns):
        base = s * stride
        if seg > tb:
            at_start = ((i * tb - row0) & (seg - 1)) == 0
            at_end = ((i * tb + tb - row0) & (seg - 1)) == 0
            scr[pl.ds(base, hal), :] = jnp.where(at_start, 0.0, prev_ref[...])
            scr[pl.ds(base + hal + segb, hal), :] = jnp.where(at_end, 0.0, next_ref[...])
        else:
            scr[pl.ds(base, hal), :] = jnp.zeros((hal, cb), F32)
            scr[pl.ds(base + hal + segb, hal), :] = jnp.zeros((hal, cb), F32)
        scr[pl.ds(base + hal, segb), :] = cur_ref[pl.ds(s * segb, segb), :]


def _conv_tiles(tb, hal, seg):
    segb, ns, stride = _conv_layout(seg, tb, hal)
    rb = min(CONV_ROWS, segb)
    return [(s * stride + hal + r0, s * segb + r0, rb) for s in range(ns) for r0 in range(0, segb, rb)]


def _conv_specs(tb, hal, cb, nrb, col0):
    q = tb // hal
    nh = nrb * q
    return [pl.BlockSpec((hal, cb), lambda j, i: (jnp.maximum(i * q - 1, 0), col0 + j)),
            pl.BlockSpec((tb, cb), lambda j, i: (i, col0 + j)),
            pl.BlockSpec((hal, cb), lambda j, i: (jnp.minimum((i + 1) * q, nh - 1), col0 + j))]


def _conv_fwd(name, x, xcol0, C, w, b, lc, seg_c, seg_l, out=None, ocol0=0):
    R = x.shape[0]
    K = w.shape[0]
    tb, hal = _conv_geometry(K, lc)
    cb = _pick(C, (512, 256, 128))
    nrb = R // tb
    wp = jnp.zeros((-(-K // SUBLANES) * SUBLANES, C), F32).at[:K].set(w)

    def body(*refs):
        prev_ref, cur_ref, next_ref, w_ref, b_ref = refs[:5]
        o_ref, scr = refs[-2], refs[-1]
        i = pl.program_id(1)

        def part(seg, row0):
            _conv_fill(scr, prev_ref, cur_ref, next_ref, i, tb, hal, seg, row0)
            for src, dst, rb in _conv_tiles(tb, hal, seg):
                acc = jnp.broadcast_to(b_ref[...], (rb, cb))
                for k in range(K):
                    acc = acc + scr[pl.ds(src + k - K // 2, rb), :] * w_ref[pl.ds(k, 1), :]
                o_ref[pl.ds(dst, rb), :] = acc

        _conv_parts(i, tb, lc, seg_c, seg_l, part)

    in_specs = _conv_specs(tb, hal, cb, nrb, xcol0 // cb)
    in_specs += [pl.BlockSpec((wp.shape[0], cb), lambda j, i: (0, j)), pl.BlockSpec((1, cb), lambda j, i: (0, j))]
    args = [x, x, x, wp, b.reshape(1, C)]
    aliases = {}
    if out is not None:
        in_specs.append(pl.BlockSpec(memory_space=pl.ANY))
        args.append(out)
        aliases = {5: 0}
    oc0 = ocol0 // cb
    return pl.pallas_call(
        body, grid=(C // cb, nrb), in_specs=in_specs,
        out_specs=pl.BlockSpec((tb, cb), lambda j, i: (i, oc0 + j)),
        out_shape=SDS((R, out.shape[1] if out is not None else C), F32),
        scratch_shapes=[pltpu.VMEM((_conv_scr_rows(tb, hal, seg_c, seg_l), cb), F32)], input_output_aliases=aliases,
        name=name, compiler_params=_cp(("parallel", "arbitrary")),
    )(*args)


def _conv_bwd_w(name, x, xcol0, dy, w_shape, lc, seg_c, seg_l):
    R = x.shape[0]
    K, C = w_shape
    tb, hal = _conv_geometry(K, lc)
    cb = _pick(C, (512, 256, 128))
    nrb = R // tb
    kp = -(-(K + 1) // SUBLANES) * SUBLANES

    def body(prev_ref, cur_ref, next_ref, dy_ref, o_ref, scr, acc8):
        i = pl.program_id(1)

        @pl.when(i == 0)
        def _():
            acc8[...] = jnp.zeros_like(acc8)

        def fold(t):
            return jnp.sum(t.reshape(t.shape[0] // SUBLANES, SUBLANES, cb), axis=0)

        def part(seg, row0):
            _conv_fill(scr, prev_ref, cur_ref, next_ref, i, tb, hal, seg, row0)
            for src, dst, rb in _conv_tiles(tb, hal, seg):
                dy_t = dy_ref[pl.ds(dst, rb), :]
                for k in range(K):
                    acc8[pl.ds(SUBLANES * k, SUBLANES), :] += fold(scr[pl.ds(src + k - K // 2, rb), :] * dy_t)
                acc8[pl.ds(SUBLANES * K, SUBLANES), :] += fold(dy_t)

        _conv_parts(i, tb, lc, seg_c, seg_l, part)

        @pl.when(i == nrb - 1)
        def _():
            o_ref[...] = jnp.zeros_like(o_ref)
            for k in range(K + 1):
                o_ref[pl.ds(k, 1), :] = jnp.sum(acc8[pl.ds(SUBLANES * k, SUBLANES), :], axis=0, keepdims=True)

    in_specs = _conv_specs(tb, hal, cb, nrb, xcol0 // cb)
    in_specs.append(pl.BlockSpec((tb, cb), lambda j, i: (i, j)))
    res = pl.pallas_call(
        body, grid=(C // cb, nrb), in_specs=in_specs,
        out_specs=pl.BlockSpec((kp, cb), lambda j, i: (0, j)),
        out_shape=SDS((kp, C), F32),
        scratch_shapes=[pltpu.VMEM((_conv_scr_rows(tb, hal, seg_c, seg_l), cb), F32),
                        pltpu.VMEM((SUBLANES * (K + 1), cb), F32)],
        name=name, compiler_params=_cp(("parallel", "arbitrary")),
    )(x, x, x, dy)
    return res[:K], res[K]


def _bdot(dims):
    (ca, cb) = dims

    def raw(a, b, dn):
        return lax.dot_general(a.astype(BF16), b.astype(BF16), (dn, ((), ())), preferred_element_type=F32)

    @jax.custom_vjp
    def f(a, b):
        return raw(a, b, ((ca,), (cb,)))

    def fwd(a, b):
        return f(a, b), (a, b)

    def bwd(res, g):
        a, b = res
        if ca == 1:
            da = raw(g, b, ((1,), (1 - cb,)))
        else:
            da = raw(b, g, ((1 - cb,), (1,)))
        if cb == 0:
            db = raw(a, g, ((1 - ca,), (0,)))
        else:
            db = raw(g, a, ((0,), (1 - ca,)))
        return da, db

    f.defvjp(fwd, bwd)
    return f


_dot_nn = _bdot((1, 0))
_dot_nt = _bdot((1, 1))
_dot_tn = _bdot((0, 0))


def _split3(x):
    hi = x.astype(BF16)
    r1 = x - hi.astype(F32)
    mid = r1.astype(BF16)
    lo = (r1 - mid.astype(F32)).astype(BF16)
    return hi, mid, lo


def _tri_raw(tri, x):
    acc = None
    for part in _split3(x):
        t = jnp.dot(tri, part, preferred_element_type=F32)
        acc = t if acc is None else acc + t
    return acc


@jax.custom_vjp
def _tri_mm(tri, x):
    return _tri_raw(tri, x)


def _tri_mm_fwd(tri, x):
    return _tri_raw(tri, x), tri


def _tri_mm_bwd(tri, g):
    return jnp.zeros_like(tri), _tri_raw(tri.T, g)


_tri_mm.defvjp(_tri_mm_fwd, _tri_mm_bwd)


def _softplus(x):
    return jnp.maximum(x, 0.0) + jnp.log(1.0 + jnp.exp(-jnp.abs(x)))


def _dt_chunk(raw, bias, alog, nheads):
    q = raw.shape[0]
    dt = _softplus(raw + bias)
    dta = dt * (-jnp.exp(alog))
    li = lax.broadcasted_iota(jnp.int32, (q, q), 0)
    si = lax.broadcasted_iota(jnp.int32, (q, q), 1)
    lower = (si <= li).astype(BF16)
    upper = (si >= li).astype(BF16)
    col = lax.broadcasted_iota(jnp.int32, raw.shape, 1)
    cum = jnp.where(col < nheads, _tri_mm(lower, dta), _tri_mm(upper, dta))
    return dt.T, cum.T


def _dt_prep(name, proj, col_block, bias, alog, nheads):
    R = proj.shape[0]
    q = CHUNK

    def body(p_ref, b_ref, a_ref, dt_ref, cum_ref):
        dtt, cumt = _dt_chunk(p_ref[...], b_ref[...], a_ref[...], nheads)
        dt_ref[...] = dtt
        cum_ref[...] = cumt

    return pl.pallas_call(
        body, grid=(R // q,),
        in_specs=[pl.BlockSpec((q, LANES), lambda i: (i, col_block)),
                  pl.BlockSpec((1, LANES), lambda i: (0, 0)), pl.BlockSpec((1, LANES), lambda i: (0, 0))],
        out_specs=[pl.BlockSpec((LANES, q), lambda i: (0, i))] * 2,
        out_shape=[SDS((LANES, R), F32)] * 2, name=name, compiler_params=_cp(("parallel",)),
    )(proj, bias, alog)


def _dt_prep_bwd(name, proj, col_block, bias, alog, nheads, ddt, dcum, out, out_col_block):
    R = proj.shape[0]
    q = CHUNK
    wide = 4 * LANES

    def body(p_ref, b_ref, a_ref, g1_ref, g2_ref, out_any, o_ref, db_ref, da_ref):
        i = pl.program_id(0)
        _, vjp = jax.vjp(lambda r, b, a: _dt_chunk(r, b, a, nheads), p_ref[...], b_ref[...], a_ref[...])
        draw, db, da = vjp((g1_ref[...], g2_ref[...]))
        o_ref[...] = jnp.concatenate([draw, jnp.zeros((q, wide - LANES), F32)], axis=1)

        @pl.when(i == 0)
        def _():
            db_ref[...] = jnp.zeros_like(db_ref)
            da_ref[...] = jnp.zeros_like(da_ref)

        db_ref[...] += db
        da_ref[...] += da

    return pl.pallas_call(
        body, grid=(R // q,),
        in_specs=[pl.BlockSpec((q, LANES), lambda i: (i, col_block)),
                  pl.BlockSpec((1, LANES), lambda i: (0, 0)), pl.BlockSpec((1, LANES), lambda i: (0, 0)),
                  pl.BlockSpec((LANES, q), lambda i: (0, i)), pl.BlockSpec((LANES, q), lambda i: (0, i)),
                  pl.BlockSpec(memory_space=pl.ANY)],
        out_specs=[pl.BlockSpec((q, wide), lambda i: (i, out_col_block)),
                   pl.BlockSpec((1, LANES), lambda i: (0, 0)), pl.BlockSpec((1, LANES), lambda i: (0, 0))],
        out_shape=[SDS(out.shape, F32), SDS((1, LANES), F32), SDS((1, LANES), F32)],
        input_output_aliases={5: 0}, name=name, compiler_params=_cp(("arbitrary",)),
    )(proj, bias, alog, ddt, dcum, out)


def _ssd_chunk(rev, xpre, bpre, cpre, dt_rows, a_rows, hin, dvec):
    q, gw = xpre.shape
    hg = len(dt_rows)
    x, bm, cm = _silu(xpre), _silu(bpre), _silu(cpre)
    li = lax.broadcasted_iota(jnp.int32, (q, q), 0)
    si = lax.broadcasted_iota(jnp.int32, (q, q), 1)
    mask = (li <= si) if rev else (li >= si)
    lane = lax.broadcasted_iota(jnp.int32, (q, LANES), 1)
    lane1 = lax.broadcasted_iota(jnp.int32, (1, LANES), 1)
    laneq = lax.broadcasted_iota(jnp.int32, (1, q), 1)
    end = 0 if rev else q - 1
    scores = _dot_nt(cm, bm)
    dt_cols = [jnp.broadcast_to(r, (LANES, q)).T for r in dt_rows]
    a_cols = [jnp.broadcast_to(r, (LANES, q)).T for r in a_rows]
    tot = [jnp.sum(jnp.where(laneq == end, r, 0.0), axis=1, keepdims=True) for r in a_rows]
    pairs = hg // 2

    def expand(cols, n):
        return jnp.concatenate(
            [jnp.where((lane if n == q else lane1) < HEAD_P, cols[2 * p], cols[2 * p + 1]) for p in range(pairs)], axis=1)

    dt_exp = expand(dt_cols, q)
    a_exp = expand(a_cols, q)
    tot_exp = expand([jnp.broadcast_to(t, (1, LANES)) for t in tot], 1)
    xt = x * dt_exp
    ys = []
    for p in range(pairs):
        xp = xt[:, p * LANES:(p + 1) * LANES]
        acc = None
        for t in range(2):
            j = 2 * p + t
            seg = a_cols[j] - a_rows[j]
            m = scores * jnp.exp(jnp.where(mask, seg, -1e30))
            xm = jnp.where((lane < HEAD_P) if t == 0 else (lane >= HEAD_P), xp, 0.0)
            part = _dot_nn(m, xm)
            acc = part if acc is None else acc + part
        ys.append(acc)
    y_diag = jnp.concatenate(ys, axis=1)
    states = _dot_tn(bm, xt * jnp.exp(tot_exp - a_exp))
    hout = jnp.exp(tot_exp) * hin + states
    y_off = _dot_nn(cm, hin) * jnp.exp(a_exp)
    return y_diag + y_off + dvec * x, hout


SSD_GROUPS_PER_STEP = 4


def _chunk_order(rev, ncc, nct):
    if not rev:
        return lambda t: t
    return lambda t: jnp.where(t < ncc, ncc - 1 - t, nct - 1 - (t - ncc))


def _ssd_fwd(name, rev, xs, bs, cs, dtt, cumt, dvec, nheads, lc):
    R, DI = xs.shape
    q, n = CHUNK, N_STATE
    G = N_GROUPS
    gw = DI // G
    hg = gw // HEAD_P
    nct, ncc = R // q, lc // q
    order = _chunk_order(rev, ncc, nct)
    gpb = SSD_GROUPS_PER_STEP

    def body(x_ref, b_ref, c_ref, dt_ref, cum_ref, d_ref, y_ref, st_ref, h_scr):
        gb, t = pl.program_id(0), pl.program_id(1)

        @pl.when(t == 0)
        def _():
            h_scr[...] = jnp.zeros_like(h_scr)

        for gi in range(gpb):
            base = (nheads if rev else 0) + (gb * gpb + gi) * hg
            dt_rows = [dt_ref[pl.ds(base + j, 1), :] for j in range(hg)]
            a_rows = [cum_ref[pl.ds(base + j, 1), :] for j in range(hg)]
            cols, ncols = slice(gi * gw, (gi + 1) * gw), slice(gi * n, (gi + 1) * n)
            hin = h_scr[gi]
            st_ref[gi] = hin
            y, hout = _ssd_chunk(rev, x_ref[:, cols], b_ref[:, ncols], c_ref[:, ncols], dt_rows, a_rows, hin,
                                 d_ref[:, cols])
            y_ref[:, cols] = y
            h_scr[gi] = hout

    return pl.pallas_call(
        body, grid=(G // gpb, nct),
        in_specs=[pl.BlockSpec((q, gpb * gw), lambda g, t: (order(t), g)),
                  pl.BlockSpec((q, gpb * n), lambda g, t: (order(t), g)),
                  pl.BlockSpec((q, gpb * n), lambda g, t: (order(t), g)),
                  pl.BlockSpec((LANES, q), lambda g, t: (0, order(t))),
                  pl.BlockSpec((LANES, q), lambda g, t: (0, order(t))),
                  pl.BlockSpec((1, gpb * gw), lambda g, t: (0, g))],
        out_specs=[pl.BlockSpec((q, gpb * gw), lambda g, t: (order(t), g)),
                   pl.BlockSpec((None, gpb, n, gw), lambda g, t: (order(t), g, 0, 0))],
        out_shape=[SDS((R, DI), F32), SDS((nct, G, n, gw), F32)],
        scratch_shapes=[pltpu.VMEM((gpb, n, gw), F32)],
        name=name, compiler_params=_cp(("parallel", "arbitrary")),
    )(xs, bs, cs, dtt, cumt, dvec)


def _ssd_bwd(name, rev, xs, bs, cs, dtt, cumt, dvec, st, dy, nheads, lc, acc=None):
    R, DI = xs.shape
    q, n = CHUNK, N_STATE
    G = N_GROUPS
    gw = DI // G
    hg = gw // HEAD_P
    hgp = -(-hg // SUBLANES) * SUBLANES
    nct, ncc = R // q, lc // q
    fwd_order = _chunk_order(rev, ncc, nct)
    order = lambda t: fwd_order(nct - 1 - t)
    has_acc = acc is not None
    gpb = SSD_GROUPS_PER_STEP

    def body(*refs):
        x_ref, b_ref, c_ref, dt_ref, cum_ref, d_ref, st_ref, dy_ref = refs[:8]
        k = 8
        acc_refs = refs[k:k + 3] if has_acc else None
        k += 3 if has_acc else 0
        dx_ref, db_ref, dc_ref, ddt_ref, dcum_ref, dd_ref, dh_scr = refs[k:]
        gb, t = pl.program_id(0), pl.program_id(1)

        @pl.when(t == 0)
        def _():
            dh_scr[...] = jnp.zeros_like(dh_scr)
            dd_ref[...] = jnp.zeros_like(dd_ref)

        if hgp > hg:
            ddt_ref[...] = jnp.zeros_like(ddt_ref)
            dcum_ref[...] = jnp.zeros_like(dcum_ref)
        for gi in range(gpb):
            base = (nheads if rev else 0) + (gb * gpb + gi) * hg
            dt_rows = [dt_ref[pl.ds(base + j, 1), :] for j in range(hg)]
            a_rows = [cum_ref[pl.ds(base + j, 1), :] for j in range(hg)]
            cols, ncols = slice(gi * gw, (gi + 1) * gw), slice(gi * n, (gi + 1) * n)
            _, vjp = jax.vjp(functools.partial(_ssd_chunk, rev), x_ref[:, cols], b_ref[:, ncols], c_ref[:, ncols],
                             dt_rows, a_rows, st_ref[gi], d_ref[:, cols])
            dx, db, dc, ddt, da, dhin, dd = vjp((dy_ref[:, cols], dh_scr[gi]))
            if has_acc:
                dx, db, dc = dx + acc_refs[0][:, cols], db + acc_refs[1][:, ncols], dc + acc_refs[2][:, ncols]
            dx_ref[:, cols] = dx
            db_ref[:, ncols] = db
            dc_ref[:, ncols] = dc
            for j in range(hg):
                ddt_ref[gi, pl.ds(j, 1), :] = ddt[j]
                dcum_ref[gi, pl.ds(j, 1), :] = da[j]
            dd_ref[:, cols] += dd
            dh_scr[gi] = dhin

    row_specs = [pl.BlockSpec((q, gpb * gw), lambda g, t: (order(t), g)),
                 pl.BlockSpec((q, gpb * n), lambda g, t: (order(t), g)),
                 pl.BlockSpec((q, gpb * n), lambda g, t: (order(t), g))]
    in_specs = row_specs + [
        pl.BlockSpec((LANES, q), lambda g, t: (0, order(t))),
        pl.BlockSpec((LANES, q), lambda g, t: (0, order(t))),
        pl.BlockSpec((1, gpb * gw), lambda g, t: (0, g)),
        pl.BlockSpec((None, gpb, n, gw), lambda g, t: (order(t), g, 0, 0)),
        pl.BlockSpec((q, gpb * gw), lambda g, t: (order(t), g))]
    args = [xs, bs, cs, dtt, cumt, dvec, st, dy]
    aliases = {}
    if has_acc:
        in_specs += row_specs
        args += list(acc)
        aliases = {8: 0, 9: 1, 10: 2}
    return pl.pallas_call(
        body, grid=(G // gpb, nct), in_specs=in_specs,
        out_specs=row_specs + [pl.BlockSpec((gpb, hgp, q), lambda g, t: (g, 0, order(t))),
                               pl.BlockSpec((gpb, hgp, q), lambda g, t: (g, 0, order(t))),
                               pl.BlockSpec((1, gpb * gw), lambda g, t: (0, g))],
        out_shape=[SDS(xs.shape, F32), SDS(bs.shape, F32), SDS(cs.shape, F32),
                   SDS((G, hgp, R), F32), SDS((G, hgp, R), F32), SDS((1, DI), F32)],
        scratch_shapes=[pltpu.VMEM((gpb, n, gw), F32)], input_output_aliases=aliases,
        name=name, compiler_params=_cp(("parallel", "arbitrary")),
    )(*args)


def _row_tile(width):
    return 256 if width <= 2048 else 128


def _rowwise_op(name, f, n_rows, out_widths, out_dtype=F32):
    def fwd(*args):
        rows, vecs = args[:n_rows], args[n_rows:]
        R = rows[0].shape[0]
        tm = min(R, _row_tile(max([a.shape[1] for a in rows] + list(out_widths))))
        outs = _rowwise(name, lambda r0, rt, vt: (f(r0, rt, vt), []), R, tm,
                        [(a, a.shape[1], 0) for a in rows], list(vecs),
                        [(w, out_dtype, None, 0, None) for w in out_widths])
        return (outs[0] if len(outs) == 1 else tuple(outs)), args

    def bwd(args, ct):
        rows, vecs = args[:n_rows], args[n_rows:]
        cts = [ct] if len(out_widths) == 1 else list(ct)
        R = rows[0].shape[0]
        tm = min(R, _row_tile(max([a.shape[1] for a in rows] + list(out_widths))))
        fb = _vjp_fn(f, n_rows, [True] * n_rows, [True] * len(vecs))
        res = _rowwise(name + "_bwd", fb, R, tm, [(a, a.shape[1], 0) for a in list(rows) + cts], list(vecs),
                       [(a.shape[1], F32, None, 0, None) for a in rows], [v.shape for v in vecs])
        return tuple(res)

    return _cvjp(fwd, bwd)


def _norm_mod_op(name, lc):
    def f(row0, rows, vecs):
        (h,), (g, shc, scc, shl, scl) = rows, vecs
        ctx = _is_ctx(row0, h.shape[0], lc)
        return [_rms(h, g) * (1.0 + jnp.where(ctx, scc, scl)) + jnp.where(ctx, shc, shl)]

    def fwd(h, *vecs):
        R, D = h.shape
        (u,) = _rowwise(name, lambda r0, rt, vt: (f(r0, rt, vt), []), R, min(R, _row_tile(D)),
                        [(h, D, 0)], list(vecs), [(D, BF16, None, 0, None)])
        return (u, h), (h,) + vecs

    def bwd(res, ct):
        h, vecs = res[0], res[1:]
        du, dhp = ct
        R, D = h.shape
        fb = _vjp_fn(f, 1, [True], [True] * 5, add_last=True)
        out = _rowwise(name + "_bwd", fb, R, min(R, _row_tile(D)), [(h, D, 0), (du, D, 0), (dhp, D, 0)], list(vecs),
                       [(D, F32, None, 0, None)], [v.shape for v in vecs])
        return tuple(out)

    return _cvjp(fwd, bwd)


def _post_res_op(name, lc):
    def branch(row0, rows, vecs):
        (y,), (g, gc, gl, bias) = rows, vecs
        return [jnp.where(_is_ctx(row0, y.shape[0], lc), gc, gl) * _rms(y + bias, g)]

    def full(row0, rows, vecs):
        return [rows[0] + branch(row0, rows[1:], vecs)[0]]

    def fwd(h, y, *vecs):
        R, D = h.shape
        (out,) = _rowwise(name, lambda r0, rt, vt: (full(r0, rt, vt), []), R, min(R, _row_tile(D)),
                          [(h, D, 0), (y, D, 0)], list(vecs), [(D, F32, None, 0, None)])
        return out, (y,) + vecs

    def bwd(res, d):
        y, vecs = res[0], res[1:]
        R, D = y.shape
        fb = _vjp_fn(branch, 1, [True], [True] * 4)
        out = _rowwise(name + "_bwd", fb, R, min(R, _row_tile(D)), [(y, D, 0), (d, D, 0)], list(vecs),
                       [(D, F32, None, 0, None)], [v.shape for v in vecs])
        return (d,) + tuple(out)

    return _cvjp(fwd, bwd)


def _payload(w, wd):
    layout, (k, ns), _ = wd
    wb = w.astype(BF16)
    return wb.reshape(k * ns // LANES, LANES) if layout == "flat" else wb


def _as_b3(g, wd):
    layout, (k, ns), npad = wd
    if layout == "col":
        return g
    if layout == "row":
        return g.reshape(1, N_DEV * k, ns)
    full = jnp.transpose(g.reshape(N_DEV, k, ns), (1, 0, 2)).reshape(k, N_DEV * ns)
    return jnp.pad(full, ((0, 0), (0, npad - N_DEV * ns)))[None]


def _as_blocks(dw3, wd):
    layout, (k, ns), _ = wd
    if layout == "col":
        return dw3
    if layout == "row":
        return dw3.reshape(N_DEV, k, ns)
    p = dw3[0, :, :N_DEV * ns].reshape(k, N_DEV, ns)
    return jnp.transpose(p, (1, 0, 2)).reshape(N_DEV, k * ns // LANES, LANES)


def _n_shards(wd):
    return N_DEV if wd[0] == "col" else 1


def _head_op(name, wd):
    def fwd(w):
        return _as_b3(_all_gather(name, _payload(w, wd)), wd), ()

    def bwd(_, db):
        return (_reduce_scatter(name + "_rs", _as_blocks(db, wd)).reshape(wd[1]),)

    return _cvjp(fwd, bwd)


def _chain_op(name, wd_cur, wd_next, next_exchanged):
    def fwd(a, b, wn):
        out, g = _matmul(name, "nn", a, b, rider=("gather", _payload(wn, wd_next)))
        return (out, _as_b3(g, wd_next)), (a, b)

    def bwd(res, ct):
        a, b = res
        dout, dbn = ct
        if next_exchanged:
            da, recv = _matmul(name + "_da_plain", "nt", dout, b, out_dtype=a.dtype), dbn
        else:
            da, recv = _matmul(name + "_da", "nt", dout, b, out_dtype=a.dtype,
                               rider=("exchange", _as_blocks(dbn, wd_next)))
        dw = _matmul(name + "_dw", "tn", a, dout, S=_n_shards(wd_cur), out_dtype=BF16)
        return da, dw, _sum_blocks(name + "_rsum", recv).reshape(wd_next[1])

    return _cvjp(fwd, bwd)


def _mlp_op(name, wd2, wd_next):
    has_next = wd_next is not None

    def fwd(u, b1, w2, *wn):
        act, g2 = _matmul(name + "_up", "nn", u, b1, out_dtype=BF16,
                          epilogue=lambda acc: jnp.square(jnp.maximum(acc, 0.0)), rider=("gather", _payload(w2, wd2)))
        b2 = _as_b3(g2, wd2)
        if has_next:
            f, gn = _matmul(name + "_down", "nn", act, b2, rider=("gather", _payload(wn[0], wd_next)))
            return (f, _as_b3(gn, wd_next)), (u, act, b1, b2)
        return _matmul(name + "_down_last", "nn", act, b2), (u, act, b1, b2)

    def bwd(res, ct):
        u, act, b1, b2 = res
        dact = dict(out_dtype=BF16, extra=act, epilogue=lambda acc, a: acc * (2.0 * jnp.sqrt(a.astype(F32))))
        if has_next:
            df, dbn = ct
            da, recvn = _matmul(name + "_dact", "nt", df, b2, rider=("exchange", _as_blocks(dbn, wd_next)), **dact)
        else:
            df = ct
            da = _matmul(name + "_dact_last", "nt", df, b2, **dact)
        dw2 = _matmul(name + "_dw2", "tn", act, df, S=1, out_dtype=BF16)
        dw1, recv2 = _matmul(name + "_dw1", "tn", u, da, S=N_DEV, out_dtype=BF16,
                             rider=("exchange", _as_blocks(dw2, wd2)))
        du, recv1 = _matmul(name + "_du", "nt", da, b1, out_dtype=u.dtype, rider=("exchange", dw1))
        out = (du, recv1, _sum_blocks(name + "_rsum2", recv2).reshape(wd2[1]))
        if has_next:
            out += (_sum_blocks(name + "_rsumn", recvn).reshape(wd_next[1]),)
        return out

    return _cvjp(fwd, bwd)


def _conv_op(name, lc, seg_c, seg_l):
    def fwd(v, w, b):
        return _conv_fwd(name, v, 0, v.shape[1], w, b, lc, seg_c, seg_l), (v, w)

    def bwd(res, dy):
        v, w = res
        C = v.shape[1]
        dv = _conv_fwd(name + "_dx", dy, 0, C, w[::-1], jnp.zeros((C,), F32), lc, seg_c, seg_l)
        dw, db = _conv_bwd_w(name + "_dw", v, 0, dy, w.shape, lc, seg_c, seg_l)
        return dv, dw, db

    return _cvjp(fwd, bwd)


def _ssm_core_op(name, cfg):
    DI, GN, H, lc, L, NP = cfg["DI"], cfg["GN"], cfg["H"], cfg["Lc"], cfg["L"], cfg["NP"]
    parts = [(DI, DI), (2 * DI, GN), (2 * DI + GN, GN)]
    dt_col = 2 * DI + 2 * GN

    def gated(row0, rows, vecs):
        (yf, yb, z), (g,) = rows, vecs
        return [_rms((yf + yb) * _silu(z), g)]

    def fwd(proj, cw, cb, alog, bias, d_f, d_b, ng):
        R = proj.shape[0]
        pre, off = [], 0
        for idx, (col, wd) in enumerate(parts):
            pre.append(_conv_fwd(f"{name}_conv{idx}", proj, col, wd, cw[:, off:off + wd], cb[off:off + wd], lc, lc, L))
            off += wd
        dtt, cumt = _dt_prep(name + "_dt", proj, dt_col // LANES, bias, alog, H)
        yf, stf = _ssd_fwd(name + "_scan_f", False, *pre, dtt, cumt, d_f, H, lc)
        yb, stb = _ssd_fwd(name + "_scan_b", True, *pre, dtt, cumt, d_b, H, lc)
        tm = min(R, _row_tile(DI))
        (out,) = _rowwise(name + "_gate", lambda r0, rt, vt: (gated(r0, rt, vt), []), R, tm,
                          [(yf, DI, 0), (yb, DI, 0), (proj, DI, 0)], [ng], [(DI, BF16, None, 0, None)])
        return out, (proj, cw, alog, bias, d_f, d_b, ng, pre, dtt, cumt, yf, yb, stf, stb)

    def bwd(res, dout):
        proj, cw, alog, bias, d_f, d_b, ng, pre, dtt, cumt, yf, yb, stf, stb = res
        R = proj.shape[0]
        tm = min(R, _row_tile(DI))
        fb = _vjp_fn(gated, 3, [True, False, True], [True])
        dy, dproj, dng = _rowwise(name + "_gate_bwd", fb, R, tm,
                                  [(yf, DI, 0), (yb, DI, 0), (proj, DI, 0), (dout, DI, 0)], [ng],
                                  [(DI, F32, None, 0, None), (DI, F32, NP, 0, None)], [ng.shape])
        rf = _ssd_bwd(name + "_scan_f_bwd", False, *pre, dtt, cumt, d_f, stf, dy, H, lc)
        rb = _ssd_bwd(name + "_scan_b_bwd", True, *pre, dtt, cumt, d_b, stb, dy, H, lc, acc=rf[:3])
        hg = DI // N_GROUPS // HEAD_P

        def head_rows(t3):
            return t3[:, :hg].reshape(H, R)

        zpad = jnp.zeros((LANES - 2 * H, R), F32)
        ddt = jnp.concatenate([head_rows(rf[3]), head_rows(rb[3]), zpad], axis=0)
        dcum = jnp.concatenate([head_rows(rf[4]), head_rows(rb[4]), zpad], axis=0)
        dproj, dbias, dalog = _dt_prep_bwd(name + "_dt_bwd", proj, dt_col // LANES, bias, alog, H, ddt, dcum,
                                           dproj, dt_col // (4 * LANES))
        dws, dbs, off = [], [], 0
        for idx, (col, wd) in enumerate(parts):
            dpart = rb[idx]
            dproj = _conv_fwd(f"{name}_conv{idx}_dx", dpart, 0, wd, cw[::-1, off:off + wd], jnp.zeros((wd,), F32),
                              lc, lc, L, out=dproj, ocol0=col)
            dw, db = _conv_bwd_w(f"{name}_conv{idx}_dw", proj, col, dpart, (cw.shape[0], wd), lc, lc, L)
            dws.append(dw)
            dbs.append(db)
            off += wd
        return (dproj, jnp.concatenate(dws, axis=1), jnp.concatenate(dbs), dalog, dbias, rf[5], rb[5], dng)

    return _cvjp(fwd, bwd)


def _mod_op(name):
    def fwd(s8, w, b):
        depth, d, ns = w.shape
        me = _lin(_my_pos())
        rows = N_DEV * SUBLANES
        s_all = _all_gather(name + "_s", s8).reshape(rows, d)
        wb = w.astype(BF16)
        bcols = lax.dynamic_slice_in_dim(b, me * ns, ns, axis=1)
        part = jnp.concatenate([_matmul(name + "_mm", "nn", s_all, wb[i][None], bias=bcols[i][None])
                                for i in range(depth)], axis=1)
        allm = _all_gather(name + "_m", part)
        mine = lax.dynamic_slice_in_dim(allm, me * SUBLANES, SUBLANES, axis=1)
        m = jnp.transpose(mine.reshape(N_DEV, SUBLANES, depth, ns), (2, 1, 0, 3)).reshape(depth, SUBLANES, N_DEV * ns)
        return m, (s_all, wb)

    def bwd(res, dm):
        s_all, wb = res
        depth, d, ns = wb.shape
        me = _lin(_my_pos())
        rows = N_DEV * SUBLANES
        fac = jnp.transpose(dm, (1, 0, 2)).reshape(SUBLANES, depth * N_DEV * ns)
        allf = _all_gather(name + "_fac", fac).reshape(rows, depth, N_DEV, ns)
        mine = lax.dynamic_slice_in_dim(allf, me, 1, axis=2)[:, :, 0]
        dws = [_matmul(name + "_dw", "tn", s_all, mine[:, i], S=1, out_dtype=F32)[0] for i in range(depth)]
        ds_part = sum(_matmul(name + "_ds", "nt", mine[:, i], wb[i][None]) for i in range(depth))
        ds_all = _all_reduce(name + "_dsr", ds_part)
        ds = lax.dynamic_slice_in_dim(ds_all, me * SUBLANES, SUBLANES, axis=0)
        db = _sum_blocks(name + "_db", allf.reshape(rows, depth * N_DEV * ns // LANES, LANES))
        return ds, jnp.stack(dws), db.reshape(depth, N_DEV * ns)

    return _cvjp(fwd, bwd)


def _loss_grad(name, h, target, lc):
    R, D = h.shape
    tm = min(lc, 256)
    off = lc // tm

    def body(h_ref, t_ref, g_ref, s_ref):
        i = pl.program_id(0)

        @pl.when(i == 0)
        def _():
            s_ref[...] = jnp.zeros_like(s_ref)

        err = jnp.where(i >= off, h_ref[...] - t_ref[...], 0.0)
        g_ref[...] = err * (1.0 / D)
        s_ref[...] += jnp.sum(err * err, axis=0, keepdims=True)

    return pl.pallas_call(
        body, grid=(R // tm,),
        in_specs=[pl.BlockSpec((tm, D), lambda i: (i, 0)),
                  pl.BlockSpec((tm, D), lambda i: (jnp.maximum(i - off, 0), 0))],
        out_specs=[pl.BlockSpec((tm, D), lambda i: (i, 0)), pl.BlockSpec((1, D), lambda i: (0, 0))],
        out_shape=[SDS((R, D), F32), SDS((1, D), F32)], name=name, compiler_params=_cp(("arbitrary",)),
    )(h, target)


def _adam(name, w, g, m, v):
    shape = w.shape
    c = shape[-1]
    r = w.size // c
    tr = _pick(r, (256, 128, 64, 32, 16, 8))
    tc = _pick(c, (2048, 1536, 1024, 768, 512, 256, 128))
    c1 = 1.0 / (1.0 - ADAM_B1 ** ADAM_STEP)
    c2 = 1.0 / (1.0 - ADAM_B2 ** ADAM_STEP)

    def body(w_ref, g_ref, m_ref, v_ref, d_ref, mo_ref, vo_ref):
        gg = g_ref[...]
        mn = ADAM_B1 * m_ref[...] + (1.0 - ADAM_B1) * gg
        vn = ADAM_B2 * v_ref[...] + (1.0 - ADAM_B2) * jnp.square(gg)
        d_ref[...] = -ADAM_LR * ((mn * c1) / (jnp.sqrt(vn * c2) + ADAM_EPS) + ADAM_WD * w_ref[...])
        mo_ref[...] = mn
        vo_ref[...] = vn

    spec = pl.BlockSpec((tr, tc), lambda i, j: (i, j))
    res = pl.pallas_call(
        body, grid=(r // tr, c // tc), in_specs=[spec] * 4, out_specs=[spec] * 3,
        out_shape=[SDS((r, c), F32)] * 3, name=name, compiler_params=_cp(("parallel", "parallel")),
    )(*[a.reshape(r, c) for a in (w, g, m, v)])
    return [a.reshape(shape) for a in res]


WEIGHTS = ["c_ctx", "mod_w", "mod_b", "pre_mix_g", "post_mix_g", "pre_mlp_g", "post_mlp_g", "mlp_w1", "mlp_w2",
           "ssm_in_w", "ssm_conv_w", "ssm_conv_b", "ssm_a_log_f", "ssm_dt_bias_f", "ssm_d_f", "ssm_a_log_b",
           "ssm_dt_bias_b", "ssm_d_b", "ssm_norm_g", "ssm_out_w", "conf_pw1_w", "conf_pw1_b", "conf_dw_w",
           "conf_dw_b", "conf_ln_g", "conf_ln_b", "conf_pw2_w", "conf_pw2_b"]
BIG_SHARDED = ["mod_w", "mlp_w1", "mlp_w2", "ssm_in_w", "ssm_out_w", "conf_pw1_w", "conf_pw2_w"]
SMALL_SHARDED = {"ssm_conv_w": 2, "conf_pw1_b": 1, "conf_dw_w": 2, "conf_dw_b": 1, "conf_ln_g": 1, "conf_ln_b": 1,
                 "conf_pw2_b": 1}


def _pack(arrs):
    flat = jnp.concatenate([a.reshape(-1).astype(F32) for a in arrs])
    n = flat.shape[0]
    tile = LANES * SUBLANES
    npad = -(-n // tile) * tile
    return jnp.pad(flat, (0, npad - n)).reshape(npad // LANES, LANES)


def _unpack(flat, shapes):
    out, off = [], 0
    for s in shapes:
        n = 1
        for d in s:
            n *= d
        out.append(flat[off:off + n].reshape(s))
        off += n
    return out


def _to_col(h, lc, rows_g):
    d = h.shape[1]
    lat = h[lc:].reshape(rows_g, GRID_W, d).swapaxes(0, 1).reshape(-1, d)
    return jnp.concatenate([h[:lc], lat], axis=0)


def _from_col(h, lc, rows_g):
    d = h.shape[1]
    lat = h[lc:].reshape(GRID_W, rows_g, d).swapaxes(0, 1).reshape(-1, d)
    return jnp.concatenate([h[:lc], lat], axis=0)


def _step(a):
    x, ctx, target = a["x"][0], a["ctx"][0], a["loss_target"][0]
    L, D = x.shape
    lc = ctx.shape[0]
    depth = a["mod_w"].shape[0]
    DI = 2 * D
    H = DI // HEAD_P
    GN = N_GROUPS * N_STATE
    cfg = dict(DI=DI, GN=GN, H=H, Lc=lc, L=L, NP=2 * DI + 2 * GN + 4 * LANES)
    rows_g = L // GRID_W
    me = _lin(_my_pos())

    names = list(SMALL_SHARDED)
    gathered = _all_gather("small_w_ag", _pack([a[n] for n in names])).reshape(N_DEV, -1)
    full_small, off = {}, 0
    for n in names:
        shp, ax = a[n].shape, SMALL_SHARDED[n]
        seg = gathered[:, off:off + a[n].size].reshape((N_DEV,) + shp)
        off += a[n].size
        full_small[n] = jnp.moveaxis(seg, 0, ax).reshape(shp[:ax] + (N_DEV * shp[ax],) + shp[ax + 1:])

    params = {n: (full_small[n] if n in SMALL_SHARDED else a[n]) for n in WEIGHTS}
    params["x"] = x

    def trunk(p):
        raw8 = jnp.concatenate([a["c"], p["c_ctx"][None], jnp.zeros((SUBLANES - 2, D), F32)], axis=0)
        s8 = _rowwise_op("silu_c", lambda r0, rows, vecs: [_silu(rows[0])], 1, [D])(raw8)
        mods = _mod_op("mod")(s8, p["mod_w"], p["mod_b"])
        zero_bias = jnp.zeros((1, D), F32)
        h = jnp.concatenate([ctx, p["x"]], axis=0)

        seq = []
        for i in range(depth):
            j = i // 2
            mixer = [("in", "ssm_in_w", "flat"), ("out", "ssm_out_w", "row")] if i % 2 == 0 else \
                    [("pw1", "conf_pw1_w", "col"), ("pw2", "conf_pw2_w", "row")]
            for role, n, layout in mixer:
                seq.append((role, p[n][j], (layout, p[n][j].shape, cfg["NP"])))
            seq.append(("w1", p["mlp_w1"][i], ("col", p["mlp_w1"][i].shape, 0)))
            seq.append(("w2", p["mlp_w2"][i], ("row", p["mlp_w2"][i].shape, 0)))
        state = {"t": 0, "b": _head_op("head_ag", seq[0][2])(seq[0][1])}

        def mm(a):
            t = state["t"]
            out, state["b"] = _chain_op(seq[t][0], seq[t][2], seq[t + 1][2], seq[t + 1][0] == "w1")(
                a, state["b"], seq[t + 1][1])
            state["t"] = t + 1
            return out

        def mlp(u2):
            t = state["t"]
            if t + 2 < len(seq):
                f, state["b"] = _mlp_op("mlp_" + seq[t + 2][0], seq[t + 1][2], seq[t + 2][2])(
                    u2, state["b"], seq[t + 1][1], seq[t + 2][1])
            else:
                f = _mlp_op("mlp", seq[t + 1][2], None)(u2, state["b"], seq[t + 1][1])
            state["t"] = t + 2
            return f

        for i in range(depth):
            kind, j = i % 2, i // 2
            col_major = (j % 2) == 1
            if i == 2:
                h = _to_col(h, lc, rows_g)
            sh1, sc1, g1, sh2, sc2, g2 = jnp.split(mods[i], 6, axis=1)
            u, hp = _norm_mod_op("pre_norm", lc)(h, p["pre_mix_g"][i][None], sh1[1:2], sc1[1:2], sh1[0:1], sc1[0:1])
            if kind == 0:
                proj = mm(u)
                pad = jnp.zeros((LANES - 2 * H,), F32)
                alog = jnp.concatenate([p["ssm_a_log_f"][j], p["ssm_a_log_b"][j], pad])[None]
                bias = jnp.concatenate([p["ssm_dt_bias_f"][j], p["ssm_dt_bias_b"][j], pad])[None]
                y = _ssm_core_op("ssm", cfg)(
                    proj, p["ssm_conv_w"][j], p["ssm_conv_b"][j], alog, bias,
                    jnp.repeat(p["ssm_d_f"][j], HEAD_P)[None], jnp.repeat(p["ssm_d_b"][j], HEAD_P)[None],
                    p["ssm_norm_g"][j][None])
                o = mm(y)
                mix_bias = zero_bias
            else:
                pre = mm(u)
                glu = _rowwise_op("glu", lambda r0, rows, vecs: [
                    (rows[0] + vecs[0])[:, :D] * jax.nn.sigmoid((rows[0] + vecs[0])[:, D:])], 1, [D])
                v = glu(pre, p["conf_pw1_b"][j][None])
                seg = rows_g if col_major else GRID_W
                cv = _conv_op(f"dw{seg}", lc, lc, seg)(v, p["conf_dw_w"][j], p["conf_dw_b"][j])

                def ln_swish(r0, rows, vecs):
                    xc = rows[0] - jnp.mean(rows[0], axis=-1, keepdims=True)
                    yn = xc * lax.rsqrt(jnp.mean(xc * xc, axis=-1, keepdims=True) + EPS) * vecs[0] + vecs[1]
                    return [_silu(yn)]

                w = _rowwise_op("ln_swish", ln_swish, 1, [D], BF16)(cv, p["conf_ln_g"][j][None], p["conf_ln_b"][j][None])
                o = mm(w)
                mix_bias = p["conf_pw2_b"][j][None]
            h = _post_res_op("post_norm", lc)(hp, o, p["post_mix_g"][i][None], g1[1:2], g1[0:1], mix_bias)
            u2, hp = _norm_mod_op("pre_norm", lc)(h, p["pre_mlp_g"][i][None], sh2[1:2], sc2[1:2], sh2[0:1], sc2[0:1])
            f = mlp(u2)
            h = _post_res_op("post_norm", lc)(hp, f, p["post_mlp_g"][i][None], g2[1:2], g2[0:1], zero_bias)
        return _from_col(h, lc, rows_g) if depth > 2 else h

    h_out, vjp = jax.vjp(trunk, params)
    dh, sq = _loss_grad("loss", h_out, target, lc)
    (grads,) = vjp(dh)
    loss = lax.psum(jnp.sum(sq) * (0.5 / D), ("x", "y", "c"))

    part_names = [n for n in WEIGHTS if n not in BIG_SHARDED and n != "mod_b"]
    red = _all_reduce("small_g", _pack([grads[n] for n in part_names])).reshape(-1)
    reduced = dict(zip(part_names, _unpack(red, [grads[n].shape for n in part_names])))
    final = {}
    for n in WEIGHTS:
        if n in BIG_SHARDED or n == "mod_b":
            final[n] = grads[n]
        elif n in SMALL_SHARDED:
            ax = SMALL_SHARDED[n]
            final[n] = lax.dynamic_slice_in_dim(reduced[n], me * a[n].shape[ax], a[n].shape[ax], axis=ax)
        else:
            final[n] = reduced[n]

    deltas, new_m, new_v = [], [], []
    for n in WEIGHTS:
        d_, m_, v_ = _adam("adam_" + n, a[n], final[n], a["m_" + n], a["v_" + n])
        deltas.append(d_)
        new_m.append(m_)
        new_v.append(v_)
    return (loss, grads["x"][None], *[final[n] for n in WEIGHTS], *deltas, *new_m, *new_v)


def kernel(x, c, ctx, c_ctx, mod_w, mod_b, pre_mix_g, post_mix_g, pre_mlp_g, post_mlp_g, mlp_w1, mlp_w2, ssm_in_w, ssm_conv_w, ssm_conv_b, ssm_a_log_f, ssm_dt_bias_f, ssm_d_f, ssm_a_log_b, ssm_dt_bias_b, ssm_d_b, ssm_norm_g, ssm_out_w, conf_pw1_w, conf_pw1_b, conf_dw_w, conf_dw_b, conf_ln_g, conf_ln_b, conf_pw2_w, conf_pw2_b, loss_target, m_c_ctx, m_mod_w, m_mod_b, m_pre_mix_g, m_post_mix_g, m_pre_mlp_g, m_post_mlp_g, m_mlp_w1, m_mlp_w2, m_ssm_in_w, m_ssm_conv_w, m_ssm_conv_b, m_ssm_a_log_f, m_ssm_dt_bias_f, m_ssm_d_f, m_ssm_a_log_b, m_ssm_dt_bias_b, m_ssm_d_b, m_ssm_norm_g, m_ssm_out_w, m_conf_pw1_w, m_conf_pw1_b, m_conf_dw_w, m_conf_dw_b, m_conf_ln_g, m_conf_ln_b, m_conf_pw2_w, m_conf_pw2_b, v_c_ctx, v_mod_w, v_mod_b, v_pre_mix_g, v_post_mix_g, v_pre_mlp_g, v_post_mlp_g, v_mlp_w1, v_mlp_w2, v_ssm_in_w, v_ssm_conv_w, v_ssm_conv_b, v_ssm_a_log_f, v_ssm_dt_bias_f, v_ssm_d_f, v_ssm_a_log_b, v_ssm_dt_bias_b, v_ssm_d_b, v_ssm_norm_g, v_ssm_out_w, v_conf_pw1_w, v_conf_pw1_b, v_conf_dw_w, v_conf_dw_b, v_conf_ln_g, v_conf_ln_b, v_conf_pw2_w, v_conf_pw2_b):
    return _step(dict(locals()))
```

```python
import functools

import jax
import jax.numpy as jnp
from jax import lax
from jax.experimental import pallas as pl
from jax.experimental.pallas import tpu as pltpu

F32, BF16 = jnp.float32, jnp.bfloat16
SDS = jax.ShapeDtypeStruct

GRID_W = 64
HEAD_P = 64
N_GROUPS = 8
N_STATE = 128
SSM_K = 5
CHUNK = 128
CONF_K = 31
EPS = 1e-6
ADAM_LR, ADAM_B1, ADAM_B2, ADAM_EPS, ADAM_WD, ADAM_STEP = 0.001, 0.9, 0.999, 1e-08, 0.01, 10

LANES = 128
SUBLANES = 8
N_DEV = 8
MESH = pl.DeviceIdType.MESH
VMEM_LIMIT = 56 * 2**20


def _cp(sem=None, vmem=VMEM_LIMIT):
    if sem is None:
        return pltpu.CompilerParams(vmem_limit_bytes=vmem)
    return pltpu.CompilerParams(dimension_semantics=sem, vmem_limit_bytes=vmem)


def _pick(dim, cands):
    for c in cands:
        if c <= dim and dim % c == 0:
            return c
    return dim


def _cvjp(fwd, bwd):
    @jax.custom_vjp
    def op(*args):
        return fwd(*args)[0]
    op.defvjp(fwd, bwd)
    return op


def _rowwise(name, fn, R, tm, rows, vecs, outs, vouts=()):
    n_r, n_v, n_o, n_vo = len(rows), len(vecs), len(outs), len(vouts)
    alias_in = [o[4] for o in outs if o[4] is not None]
    n_a = len(alias_in)

    def body(*refs):
        r = refs[:n_r]
        v = refs[n_r:n_r + n_v]
        o = refs[n_r + n_v + n_a:n_r + n_v + n_a + n_o]
        vo = refs[n_r + n_v + n_a + n_o:]
        i = pl.program_id(0)
        ro, vvals = fn(i * tm, [x[...] for x in r], [x[...] for x in v])
        for ref, val in zip(o, ro):
            ref[...] = val.astype(ref.dtype)
        if n_vo:
            @pl.when(i == 0)
            def _():
                for ref in vo:
                    ref[...] = jnp.zeros_like(ref)
            for ref, val in zip(vo, vvals):
                ref[...] += val

    def colmap(cb):
        return lambda i: (i, cb)

    in_specs = [pl.BlockSpec((tm, w), colmap(cb)) for (_, w, cb) in rows]
    in_specs += [pl.BlockSpec(a.shape, lambda i: (0, 0)) for a in vecs]
    in_specs += [pl.BlockSpec(memory_space=pl.ANY)] * n_a
    out_shape, out_specs, aliases = [], [], {}
    ai = 0
    for k, (w, dt, tot, cb, al) in enumerate(outs):
        out_shape.append(SDS((R, tot if tot else w), dt))
        out_specs.append(pl.BlockSpec((tm, w), colmap(cb)))
        if al is not None:
            aliases[n_r + n_v + ai] = k
            ai += 1
    for (vv, w) in vouts:
        out_shape.append(SDS((vv, w), F32))
        out_specs.append(pl.BlockSpec((vv, w), lambda i: (0, 0)))
    res = pl.pallas_call(
        body, grid=(R // tm,), in_specs=in_specs, out_specs=out_specs, out_shape=out_shape,
        input_output_aliases=aliases, name=name, compiler_params=_cp(("arbitrary",)),
    )(*[a for a, _, _ in rows], *vecs, *alias_in)
    return list(res)


def _vjp_fn(f, n_rows, want_rows, want_vecs, add_last=False):
    def fb(row0, tiles, vecs):
        rows = [t.astype(F32) for t in tiles[:n_rows]]
        outs, vjp = jax.vjp(lambda rs, vs: f(row0, rs, vs), rows, [v.astype(F32) for v in vecs])
        cts = [t.astype(F32) for t in tiles[n_rows:n_rows + len(outs)]]
        g_rows, g_vecs = vjp(cts)
        gr = [g for g, w in zip(g_rows, want_rows) if w]
        if add_last:
            gr[0] = gr[0] + tiles[-1].astype(F32)
        return gr, [g for g, w in zip(g_vecs, want_vecs) if w]
    return fb


def _silu(x):
    return x * jax.nn.sigmoid(x)


def _rms(x, g):
    return x * lax.rsqrt(jnp.mean(x * x, axis=-1, keepdims=True) + EPS) * g


def _is_ctx(row0, tm, lc):
    return (row0 + lax.broadcasted_iota(jnp.int32, (tm, 1), 0)) < lc


ROW_TILES = (768, 512, 384, 256, 128, 64, 32, 16, 8)
COL_TILES = (1024, 768, 512, 384, 256, 128)
NT_VMEM_BUDGET = 44 * 2**20


def _matmul(name, mode, lhs, rhs, S=1, out_dtype=F32, bias=None, epilogue=None, extra=None, rider=None):
    if mode == "nn":
        M, K = lhs.shape
        Ns = rhs.shape[2]
        t0, t1, t2 = _pick(M, ROW_TILES), _pick(Ns, COL_TILES), _pick(K, (2048, 1024, 512, 256, 128))
        per = Ns // t1
        grid = (M // t0, rhs.shape[0] * per, K // t2)
        lhs_spec = pl.BlockSpec((t0, t2), lambda i, j, k: (i, k))
        rhs_spec = pl.BlockSpec((None, t2, t1), lambda i, j, k: (j // per, k, j % per))
        out_spec = pl.BlockSpec((t0, t1), lambda i, j, k: (i, j))
        out_shape = (M, rhs.shape[0] * Ns)
        contract = ((1,), (0,))
    elif mode == "nt":
        M = lhs.shape[0]
        _, K, Ns = rhs.shape
        t0, t2 = _pick(M, ROW_TILES), _pick(Ns, (2048,) + COL_TILES)

        def vmem_bytes(t1):
            per_out = 2 * jnp.dtype(out_dtype).itemsize + 4 + (2 * extra.dtype.itemsize if extra is not None else 0)
            return 2 * (t0 * t2 * lhs.dtype.itemsize + t1 * t2 * rhs.dtype.itemsize) + t0 * t1 * per_out

        t1 = next(c for c in (2048, 1024, 512, 256, 128) if K % c == 0 and (c <= 1024 or vmem_bytes(c) <= NT_VMEM_BUDGET))
        per = Ns // t2
        grid = (M // t0, K // t1, rhs.shape[0] * per)
        lhs_spec = pl.BlockSpec((t0, t2), lambda i, j, k: (i, k))
        rhs_spec = pl.BlockSpec((None, t1, t2), lambda i, j, k: (k // per, j, k % per))
        out_spec = pl.BlockSpec((t0, t1), lambda i, j, k: (i, j))
        out_shape = (M, K)
        contract = ((1,), (1,))
    else:
        M, K = lhs.shape
        Ns = rhs.shape[1] // S
        t0, t1, t2 = _pick(K, (2048, 1024, 512, 256, 128)), _pick(Ns, COL_TILES), _pick(M, ROW_TILES)
        per = Ns // t1
        grid = (K // t0, S * per, M // t2)
        lhs_spec = pl.BlockSpec((t2, t0), lambda i, j, k: (k, i))
        rhs_spec = pl.BlockSpec((t2, t1), lambda i, j, k: (k, j))
        out_spec = pl.BlockSpec((None, t0, t1), lambda i, j, k: (j // per, i, j % per))
        out_shape = (S, K, Ns)
        contract = ((0,), (0,))
    has_bias, has_extra, has_rider = bias is not None, extra is not None, rider is not None
    n_in = 2 + has_bias + has_extra + has_rider
    g0, g1, g2 = grid

    def body(*refs):
        l_ref, r_ref = refs[0], refs[1]
        bias_ref = refs[2] if has_bias else None
        e_ref = refs[2 + has_bias] if has_extra else None
        o_ref = refs[n_in]
        acc_ref = refs[n_in + 1 + has_rider]
        i, j, k = pl.program_id(0), pl.program_id(1), pl.program_id(2)
        if has_rider:
            start, mid, finish = RIDERS[rider[0]](refs[n_in - 1], refs[n_in + 1], *refs[n_in + 3:])
            pl.when((i == 0) & (j == 0) & (k == 0))(start)
            pl.when((i == g0 // 2) & (j == 0) & (k == 0))(mid)

        @pl.when(k == 0)
        def _():
            acc_ref[...] = jnp.zeros_like(acc_ref)

        acc_ref[...] += lax.dot_general(l_ref[...].astype(BF16), r_ref[...].astype(BF16), (contract, ((), ())),
                                        preferred_element_type=F32)

        @pl.when(k == g2 - 1)
        def _():
            acc = acc_ref[...]
            if has_bias:
                acc = acc + bias_ref[...]
            if epilogue is not None:
                acc = epilogue(acc, e_ref[...]) if has_extra else epilogue(acc)
            o_ref[...] = acc.astype(o_ref.dtype)

        if has_rider:
            pl.when((i == g0 - 1) & (j == g1 - 1) & (k == g2 - 1))(finish)

    in_specs, args = [lhs_spec, rhs_spec], [lhs, rhs]
    if has_bias:
        in_specs.append(pl.BlockSpec((1, out_spec.block_shape[-1]), lambda i, j, k: (0, j)))
        args.append(bias)
    if has_extra:
        in_specs.append(out_spec)
        args.append(extra)
    out_specs, out_shapes = [out_spec], [SDS(out_shape, out_dtype)]
    scratch = [pltpu.VMEM(tuple(d for d in out_spec.block_shape if d is not None), F32)]
    if has_rider:
        in_specs.append(pl.BlockSpec(memory_space=pl.ANY))
        args.append(rider[1])
        out_specs.append(pl.BlockSpec(memory_space=pl.ANY))
        out_shapes.append(_rider_out(*rider))
        scratch += _comm_scratch()
    res = pl.pallas_call(
        body, grid=grid, in_specs=in_specs, out_specs=out_specs, out_shape=out_shapes, scratch_shapes=scratch,
        name=name, compiler_params=_cp(("arbitrary", "arbitrary", "arbitrary")),
    )(*args)
    return (res[0], res[1]) if has_rider else res[0]


def _my_pos():
    return lax.axis_index("x"), lax.axis_index("y"), lax.axis_index("c")


def _flip(pos, k):
    x, y, c = pos
    return (1 - x if k & 4 else x, 1 - y if k & 2 else y, 1 - c if k & 1 else c)


def _lin(pos):
    return 4 * pos[0] + 2 * pos[1] + pos[2]


def _all_gather(name, x):
    r, c = x.shape

    def body(x_ref, out_ref, *sems):
        start, mid, finish = _gather_steps(x_ref, out_ref, *sems)
        start()
        mid()
        finish()

    return pl.pallas_call(
        body, out_shape=SDS((N_DEV, r, c), x.dtype),
        in_specs=[pl.BlockSpec(memory_space=pl.ANY)], out_specs=pl.BlockSpec(memory_space=pl.ANY),
        scratch_shapes=_comm_scratch(), name=name, compiler_params=_cp(),
    )(x)


def _comm_scratch():
    return [pltpu.SemaphoreType.DMA((7,)), pltpu.SemaphoreType.DMA((7,)), pltpu.SemaphoreType.DMA(())]


def _gather_steps(x_ref, out_ref, send_sems, recv_sems, local_sem):
    me = _my_pos()
    sibling = _flip(me, 1)
    chips = [2, 4, 6]

    def copy(k, block, to, src=None):
        dst = out_ref.at[_lin(block)]
        return pltpu.make_async_remote_copy(
            src_ref=dst if src is None else src, dst_ref=dst,
            send_sem=send_sems.at[k], recv_sem=recv_sems.at[k], device_id=to, device_id_type=MESH)

    def mine():
        return pltpu.make_async_copy(x_ref, out_ref.at[_lin(me)], local_sem)

    def first():
        return [copy(0, me, sibling, src=x_ref)] + [copy(1 + j, me, _flip(me, m), src=x_ref) for j, m in enumerate(chips)]

    def passed():
        return [copy(4 + j, _flip(me, m), sibling) for j, m in enumerate(chips)]

    def start():
        mine().start()
        for cp in first():
            cp.start()

    def mid():
        fw = passed()
        for j, m in enumerate(chips):
            copy(1 + j, _flip(me, m), me).wait_recv()
            fw[j].start()

    def finish():
        copy(0, sibling, me).wait_recv()
        for j, m in enumerate(chips):
            copy(4 + j, _flip(sibling, m), me).wait_recv()
        for cp in first() + passed():
            cp.wait_send()
        mine().wait()

    return start, mid, finish


def _exchange_steps(p_ref, out_ref, send_sems, recv_sems, local_sem):
    me = _my_pos()
    my_id = _lin(me)

    def mine():
        return pltpu.make_async_copy(p_ref.at[my_id], out_ref.at[my_id], local_sem)

    def sends():
        out = []
        for k in range(1, N_DEV):
            peer = _flip(me, k)
            out.append(pltpu.make_async_remote_copy(
                src_ref=p_ref.at[_lin(peer)], dst_ref=out_ref.at[my_id],
                send_sem=send_sems.at[k - 1], recv_sem=recv_sems.at[k - 1], device_id=peer, device_id_type=MESH))
        return out

    def start():
        mine().start()
        for cp in sends():
            cp.start()

    def mid():
        pass

    def finish():
        for k in range(1, N_DEV):
            peer = _flip(me, k)
            pltpu.make_async_remote_copy(
                src_ref=p_ref.at[my_id], dst_ref=out_ref.at[_lin(peer)],
                send_sem=send_sems.at[k - 1], recv_sem=recv_sems.at[k - 1], device_id=peer, device_id_type=MESH).wait_recv()
        for cp in sends():
            cp.wait_send()
        mine().wait()

    return start, mid, finish


RIDERS = {"gather": _gather_steps, "exchange": _exchange_steps}


def _rider_out(kind, payload):
    return SDS((N_DEV,) + tuple(payload.shape), payload.dtype) if kind == "gather" else SDS(payload.shape, payload.dtype)


def _exchange(name, p3):
    def body(p_ref, out_ref, *sems):
        start, mid, finish = _exchange_steps(p_ref, out_ref, *sems)
        start()
        finish()

    return pl.pallas_call(
        body, out_shape=SDS(p3.shape, p3.dtype),
        in_specs=[pl.BlockSpec(memory_space=pl.ANY)], out_specs=pl.BlockSpec(memory_space=pl.ANY),
        scratch_shapes=_comm_scratch(), name=name, compiler_params=_cp(),
    )(p3)


def _sum_blocks(name, g3):
    n, r, c = g3.shape
    tr = _pick(r, (512, 256, 128, 64, 32, 16, 8))
    tc = _pick(c, (2048, 1024, 512, 256, 128))

    def body(g_ref, o_ref):
        acc = g_ref[0].astype(F32)
        for d in range(1, n):
            acc = acc + g_ref[d].astype(F32)
        o_ref[...] = acc

    return pl.pallas_call(
        body, grid=(r // tr, c // tc),
        in_specs=[pl.BlockSpec((n, tr, tc), lambda i, j: (0, i, j))],
        out_specs=pl.BlockSpec((tr, tc), lambda i, j: (i, j)),
        out_shape=SDS((r, c), F32), name=name, compiler_params=_cp(("parallel", "parallel")),
    )(g3)


def _reduce_scatter(name, p3):
    return _sum_blocks(name + "_sum", _exchange(name + "_xchg", p3))


def _all_reduce(name, v):
    return _sum_blocks(name + "_sum", _all_gather(name + "_ag", v))


def _conv_geometry(K, lc):
    tb = _pick(lc, (256, 128, 64, 32, 16, 8))
    hal = -(-(K // 2) // SUBLANES) * SUBLANES
    return tb, hal


CONV_ROWS = 32


def _conv_layout(seg, tb, hal):
    segb = min(seg, tb)
    return segb, tb // segb, segb + 2 * hal


def _conv_scr_rows(tb, hal, seg_c, seg_l):
    return max(ns * stride for _, ns, stride in (_conv_layout(seg_c, tb, hal), _conv_layout(seg_l, tb, hal)))


def _conv_parts(i, tb, lc, seg_c, seg_l, part):
    if seg_c == seg_l:
        part(seg_c, 0)
        return

    @pl.when(i * tb < lc)
    def _():
        part(seg_c, 0)

    @pl.when(i * tb >= lc)
    def _():
        part(seg_l, lc)


def _conv_fill(scr, prev_ref, cur_ref, next_ref, i, tb, hal, seg, row0):
    segb, ns, stride = _conv_layout(seg, tb, hal)
    cb = cur_ref.shape[1]
    for s in range(ns):
        base = s * stride
        if seg > tb:
            at_start = ((i * tb - row0) & (seg - 1)) == 0
            at_end = ((i * tb + tb - row0) & (seg - 1)) == 0
            scr[pl.ds(base, hal), :] = jnp.where(at_start, 0.0, prev_ref[...])
            scr[pl.ds(base + hal + segb, hal), :] = jnp.where(at_end, 0.0, next_ref[...])
        else:
            scr[pl.ds(base, hal), :] = jnp.zeros((hal, cb), F32)
            scr[pl.ds(base + hal + segb, hal), :] = jnp.zeros((hal, cb), F32)
        scr[pl.ds(base + hal, segb), :] = cur_ref[pl.ds(s * segb, segb), :]


def _conv_tiles(tb, hal, seg):
    segb, ns, stride = _conv_layout(seg, tb, hal)
    rb = min(CONV_ROWS, segb)
    return [(s * stride + hal + r0, s * segb + r0, rb) for s in range(ns) for r0 in range(0, segb, rb)]


def _conv_specs(tb, hal, cb, nrb, col0):
    q = tb // hal
    nh = nrb * q
    return [pl.BlockSpec((hal, cb), lambda j, i: (jnp.maximum(i * q - 1, 0), col0 + j)),
            pl.BlockSpec((tb, cb), lambda j, i: (i, col0 + j)),
            pl.BlockSpec((hal, cb), lambda j, i: (jnp.minimum((i + 1) * q, nh - 1), col0 + j))]


def _conv_fwd(name, x, xcol0, C, w, b, lc, seg_c, seg_l, out=None, ocol0=0):
    R = x.shape[0]
    K = w.shape[0]
    tb, hal = _conv_geometry(K, lc)
    cb = _pick(C, (512, 256, 128))
    nrb = R // tb
    wp = jnp.zeros((-(-K // SUBLANES) * SUBLANES, C), F32).at[:K].set(w)

    def body(*refs):
        prev_ref, cur_ref, next_ref, w_ref, b_ref = refs[:5]
        o_ref, scr = refs[-2], refs[-1]
        i = pl.program_id(1)

        def part(seg, row0):
            _conv_fill(scr, prev_ref, cur_ref, next_ref, i, tb, hal, seg, row0)
            for src, dst, rb in _conv_tiles(tb, hal, seg):
                acc = jnp.broadcast_to(b_ref[...], (rb, cb))
                for k in range(K):
                    acc = acc + scr[pl.ds(src + k - K // 2, rb), :] * w_ref[pl.ds(k, 1), :]
                o_ref[pl.ds(dst, rb), :] = acc

        _conv_parts(i, tb, lc, seg_c, seg_l, part)

    in_specs = _conv_specs(tb, hal, cb, nrb, xcol0 // cb)
    in_specs += [pl.BlockSpec((wp.shape[0], cb), lambda j, i: (0, j)), pl.BlockSpec((1, cb), lambda j, i: (0, j))]
    args = [x, x, x, wp, b.reshape(1, C)]
    aliases = {}
    if out is not None:
        in_specs.append(pl.BlockSpec(memory_space=pl.ANY))
        args.append(out)
        aliases = {5: 0}
    oc0 = ocol0 // cb
    return pl.pallas_call(
        body, grid=(C // cb, nrb), in_specs=in_specs,
        out_specs=pl.BlockSpec((tb, cb), lambda j, i: (i, oc0 + j)),
        out_shape=SDS((R, out.shape[1] if out is not None else C), F32),
        scratch_shapes=[pltpu.VMEM((_conv_scr_rows(tb, hal, seg_c, seg_l), cb), F32)], input_output_aliases=aliases,
        name=name, compiler_params=_cp(("parallel", "arbitrary")),
    )(*args)


def _conv_bwd_w(name, x, xcol0, dy, w_shape, lc, seg_c, seg_l):
    R = x.shape[0]
    K, C = w_shape
    tb, hal = _conv_geometry(K, lc)
    cb = _pick(C, (512, 256, 128))
    nrb = R // tb
    kp = -(-(K + 1) // SUBLANES) * SUBLANES

    def body(prev_ref, cur_ref, next_ref, dy_ref, o_ref, scr, acc8):
        i = pl.program_id(1)

        @pl.when(i == 0)
        def _():
            acc8[...] = jnp.zeros_like(acc8)

        def fold(t):
            return jnp.sum(t.reshape(t.shape[0] // SUBLANES, SUBLANES, cb), axis=0)

        def part(seg, row0):
            _conv_fill(scr, prev_ref, cur_ref, next_ref, i, tb, hal, seg, row0)
            for src, dst, rb in _conv_tiles(tb, hal, seg):
                dy_t = dy_ref[pl.ds(dst, rb), :]
                for k in range(K):
                    acc8[pl.ds(SUBLANES * k, SUBLANES), :] += fold(scr[pl.ds(src + k - K // 2, rb), :] * dy_t)
                acc8[pl.ds(SUBLANES * K, SUBLANES), :] += fold(dy_t)

        _conv_parts(i, tb, lc, seg_c, seg_l, part)

        @pl.when(i == nrb - 1)
        def _():
            o_ref[...] = jnp.zeros_like(o_ref)
            for k in range(K + 1):
                o_ref[pl.ds(k, 1), :] = jnp.sum(acc8[pl.ds(SUBLANES * k, SUBLANES), :], axis=0, keepdims=True)

    in_specs = _conv_specs(tb, hal, cb, nrb, xcol0 // cb)
    in_specs.append(pl.BlockSpec((tb, cb), lambda j, i: (i, j)))
    res = pl.pallas_call(
        body, grid=(C // cb, nrb), in_specs=in_specs,
        out_specs=pl.BlockSpec((kp, cb), lambda j, i: (0, j)),
        out_shape=SDS((kp, C), F32),
        scratch_shapes=[pltpu.VMEM((_conv_scr_rows(tb, hal, seg_c, seg_l), cb), F32),
                        pltpu.VMEM((SUBLANES * (K + 1), cb), F32)],
        name=name, compiler_params=_cp(("parallel", "arbitrary")),
    )(x, x, x, dy)
    return res[:K], res[K]


def _bdot(dims):
    (ca, cb) = dims

    def raw(a, b, dn):
        return lax.dot_general(a.astype(BF16), b.astype(BF16), (dn, ((), ())), preferred_element_type=F32)

    @jax.custom_vjp
    def f(a, b):
        return raw(a, b, ((ca,), (cb,)))

    def fwd(a, b):
        return f(a, b), (a, b)

    def bwd(res, g):
        a, b = res
        if ca == 1:
            da = raw(g, b, ((1,), (1 - cb,)))
        else:
            da = raw(b, g, ((1 - cb,), (1,)))
        if cb == 0:
            db = raw(a, g, ((1 - ca,), (0,)))
        else:
            db = raw(g, a, ((0,), (1 - ca,)))
        return da, db

    f.defvjp(fwd, bwd)
    return f


_dot_nn = _bdot((1, 0))
_dot_nt = _bdot((1, 1))
_dot_tn = _bdot((0, 0))


def _split3(x):
    hi = x.astype(BF16)
    r1 = x - hi.astype(F32)
    mid = r1.astype(BF16)
    lo = (r1 - mid.astype(F32)).astype(BF16)
    return hi, mid, lo


def _tri_raw(tri, x):
    acc = None
    for part in _split3(x):
        t = jnp.dot(tri, part, preferred_element_type=F32)
        acc = t if acc is None else acc + t
    return acc


@jax.custom_vjp
def _tri_mm(tri, x):
    return _tri_raw(tri, x)


def _tri_mm_fwd(tri, x):
    return _tri_raw(tri, x), tri


def _tri_mm_bwd(tri, g):
    return jnp.zeros_like(tri), _tri_raw(tri.T, g)


_tri_mm.defvjp(_tri_mm_fwd, _tri_mm_bwd)


def _softplus(x):
    return jnp.maximum(x, 0.0) + jnp.log(1.0 + jnp.exp(-jnp.abs(x)))


def _dt_chunk(raw, bias, alog, nheads):
    q = raw.shape[0]
    dt = _softplus(raw + bias)
    dta = dt * (-jnp.exp(alog))
    li = lax.broadcasted_iota(jnp.int32, (q, q), 0)
    si = lax.broadcasted_iota(jnp.int32, (q, q), 1)
    lower = (si <= li).astype(BF16)
    upper = (si >= li).astype(BF16)
    col = lax.broadcasted_iota(jnp.int32, raw.shape, 1)
    cum = jnp.where(col < nheads, _tri_mm(lower, dta), _tri_mm(upper, dta))
    return dt.T, cum.T


def _dt_prep(name, proj, col_block, bias, alog, nheads):
    R = proj.shape[0]
    q = CHUNK

    def body(p_ref, b_ref, a_ref, dt_ref, cum_ref):
        dtt, cumt = _dt_chunk(p_ref[...], b_ref[...], a_ref[...], nheads)
        dt_ref[...] = dtt
        cum_ref[...] = cumt

    return pl.pallas_call(
        body, grid=(R // q,),
        in_specs=[pl.BlockSpec((q, LANES), lambda i: (i, col_block)),
                  pl.BlockSpec((1, LANES), lambda i: (0, 0)), pl.BlockSpec((1, LANES), lambda i: (0, 0))],
        out_specs=[pl.BlockSpec((LANES, q), lambda i: (0, i))] * 2,
        out_shape=[SDS((LANES, R), F32)] * 2, name=name, compiler_params=_cp(("parallel",)),
    )(proj, bias, alog)


def _dt_prep_bwd(name, proj, col_block, bias, alog, nheads, ddt, dcum, out, out_col_block):
    R = proj.shape[0]
    q = CHUNK
    wide = 4 * LANES

    def body(p_ref, b_ref, a_ref, g1_ref, g2_ref, out_any, o_ref, db_ref, da_ref):
        i = pl.program_id(0)
        _, vjp = jax.vjp(lambda r, b, a: _dt_chunk(r, b, a, nheads), p_ref[...], b_ref[...], a_ref[...])
        draw, db, da = vjp((g1_ref[...], g2_ref[...]))
        o_ref[...] = jnp.concatenate([draw, jnp.zeros((q, wide - LANES), F32)], axis=1)

        @pl.when(i == 0)
        def _():
            db_ref[...] = jnp.zeros_like(db_ref)
            da_ref[...] = jnp.zeros_like(da_ref)

        db_ref[...] += db
        da_ref[...] += da

    return pl.pallas_call(
        body, grid=(R // q,),
        in_specs=[pl.BlockSpec((q, LANES), lambda i: (i, col_block)),
                  pl.BlockSpec((1, LANES), lambda i: (0, 0)), pl.BlockSpec((1, LANES), lambda i: (0, 0)),
                  pl.BlockSpec((LANES, q), lambda i: (0, i)), pl.BlockSpec((LANES, q), lambda i: (0, i)),
                  pl.BlockSpec(memory_space=pl.ANY)],
        out_specs=[pl.BlockSpec((q, wide), lambda i: (i, out_col_block)),
                   pl.BlockSpec((1, LANES), lambda i: (0, 0)), pl.BlockSpec((1, LANES), lambda i: (0, 0))],
        out_shape=[SDS(out.shape, F32), SDS((1, LANES), F32), SDS((1, LANES), F32)],
        input_output_aliases={5: 0}, name=name, compiler_params=_cp(("arbitrary",)),
    )(proj, bias, alog, ddt, dcum, out)


def _ssd_chunk(rev, xpre, bpre, cpre, dt_rows, a_rows, hin, dvec):
    q, gw = xpre.shape
    hg = len(dt_rows)
    x, bm, cm = _silu(xpre), _silu(bpre), _silu(cpre)
    li = lax.broadcasted_iota(jnp.int32, (q, q), 0)
    si = lax.broadcasted_iota(jnp.int32, (q, q), 1)
    mask = (li <= si) if rev else (li >= si)
    lane = lax.broadcasted_iota(jnp.int32, (q, LANES), 1)
    lane1 = lax.broadcasted_iota(jnp.int32, (1, LANES), 1)
    laneq = lax.broadcasted_iota(jnp.int32, (1, q), 1)
    end = 0 if rev else q - 1
    scores = _dot_nt(cm, bm)
    dt_cols = [jnp.broadcast_to(r, (LANES, q)).T for r in dt_rows]
    a_cols = [jnp.broadcast_to(r, (LANES, q)).T for r in a_rows]
    tot = [jnp.sum(jnp.where(laneq == end, r, 0.0), axis=1, keepdims=True) for r in a_rows]
    pairs = hg // 2

    def expand(cols, n):
        return jnp.concatenate(
            [jnp.where((lane if n == q else lane1) < HEAD_P, cols[2 * p], cols[2 * p + 1]) for p in range(pairs)], axis=1)

    dt_exp = expand(dt_cols, q)
    a_exp = expand(a_cols, q)
    tot_exp = expand([jnp.broadcast_to(t, (1, LANES)) for t in tot], 1)
    xt = x * dt_exp
    ys = []
    for p in range(pairs):
        xp = xt[:, p * LANES:(p + 1) * LANES]
        acc = None
        for t in range(2):
            j = 2 * p + t
            seg = a_cols[j] - a_rows[j]
            m = scores * jnp.exp(jnp.where(mask, seg, -1e30))
            xm = jnp.where((lane < HEAD_P) if t == 0 else (lane >= HEAD_P), xp, 0.0)
            part = _dot_nn(m, xm)
            acc = part if acc is None else acc + part
        ys.append(acc)
    y_diag = jnp.concatenate(ys, axis=1)
    states = _dot_tn(bm, xt * jnp.exp(tot_exp - a_exp))
    hout = jnp.exp(tot_exp) * hin + states
    y_off = _dot_nn(cm, hin) * jnp.exp(a_exp)
    return y_diag + y_off + dvec * x, hout


SSD_GROUPS_PER_STEP = 4
SSD_GROUPS_PER_STEP_FWD = 8


def _chunk_order(rev, ncc, nct):
    if not rev:
        return lambda t: t
    return lambda t: jnp.where(t < ncc, ncc - 1 - t, nct - 1 - (t - ncc))


def _ssd_fwd(name, rev, xs, bs, cs, dtt, cumt, dvec, nheads, lc):
    R, DI = xs.shape
    q, n = CHUNK, N_STATE
    G = N_GROUPS
    gw = DI // G
    hg = gw // HEAD_P
    nct, ncc = R // q, lc // q
    order = _chunk_order(rev, ncc, nct)
    gpb = SSD_GROUPS_PER_STEP_FWD

    def body(x_ref, b_ref, c_ref, dt_ref, cum_ref, d_ref, y_ref, st_ref, h_scr):
        gb, t = pl.program_id(0), pl.program_id(1)

        @pl.when(t == 0)
        def _():
            h_scr[...] = jnp.zeros_like(h_scr)

        for gi in range(gpb):
            base = (nheads if rev else 0) + (gb * gpb + gi) * hg
            dt_rows = [dt_ref[pl.ds(base + j, 1), :] for j in range(hg)]
            a_rows = [cum_ref[pl.ds(base + j, 1), :] for j in range(hg)]
            cols, ncols = slice(gi * gw, (gi + 1) * gw), slice(gi * n, (gi + 1) * n)
            hin = h_scr[gi]
            st_ref[gi] = hin
            y, hout = _ssd_chunk(rev, x_ref[:, cols], b_ref[:, ncols], c_ref[:, ncols], dt_rows, a_rows, hin,
                                 d_ref[:, cols])
            y_ref[:, cols] = y
            h_scr[gi] = hout

    return pl.pallas_call(
        body, grid=(G // gpb, nct),
        in_specs=[pl.BlockSpec((q, gpb * gw), lambda g, t: (order(t), g)),
                  pl.BlockSpec((q, gpb * n), lambda g, t: (order(t), g)),
                  pl.BlockSpec((q, gpb * n), lambda g, t: (order(t), g)),
                  pl.BlockSpec((LANES, q), lambda g, t: (0, order(t))),
                  pl.BlockSpec((LANES, q), lambda g, t: (0, order(t))),
                  pl.BlockSpec((1, gpb * gw), lambda g, t: (0, g))],
        out_specs=[pl.BlockSpec((q, gpb * gw), lambda g, t: (order(t), g)),
                   pl.BlockSpec((None, gpb, n, gw), lambda g, t: (order(t), g, 0, 0))],
        out_shape=[SDS((R, DI), F32), SDS((nct, G, n, gw), F32)],
        scratch_shapes=[pltpu.VMEM((gpb, n, gw), F32)],
        name=name, compiler_params=_cp(("parallel", "arbitrary")),
    )(xs, bs, cs, dtt, cumt, dvec)


def _ssd_bwd(name, rev, xs, bs, cs, dtt, cumt, dvec, st, dy, nheads, lc, acc=None):
    R, DI = xs.shape
    q, n = CHUNK, N_STATE
    G = N_GROUPS
    gw = DI // G
    hg = gw // HEAD_P
    hgp = -(-hg // SUBLANES) * SUBLANES
    nct, ncc = R // q, lc // q
    fwd_order = _chunk_order(rev, ncc, nct)
    order = lambda t: fwd_order(nct - 1 - t)
    has_acc = acc is not None
    gpb = SSD_GROUPS_PER_STEP

    def body(*refs):
        x_ref, b_ref, c_ref, dt_ref, cum_ref, d_ref, st_ref, dy_ref = refs[:8]
        k = 8
        acc_refs = refs[k:k + 3] if has_acc else None
        k += 3 if has_acc else 0
        dx_ref, db_ref, dc_ref, ddt_ref, dcum_ref, dd_ref, dh_scr = refs[k:]
        gb, t = pl.program_id(0), pl.program_id(1)

        @pl.when(t == 0)
        def _():
            dh_scr[...] = jnp.zeros_like(dh_scr)
            dd_ref[...] = jnp.zeros_like(dd_ref)

        if hgp > hg:
            ddt_ref[...] = jnp.zeros_like(ddt_ref)
            dcum_ref[...] = jnp.zeros_like(dcum_ref)
        for gi in range(gpb):
            base = (nheads if rev else 0) + (gb * gpb + gi) * hg
            dt_rows = [dt_ref[pl.ds(base + j, 1), :] for j in range(hg)]
            a_rows = [cum_ref[pl.ds(base + j, 1), :] for j in range(hg)]
            cols, ncols = slice(gi * gw, (gi + 1) * gw), slice(gi * n, (gi + 1) * n)
            _, vjp = jax.vjp(functools.partial(_ssd_chunk, rev), x_ref[:, cols], b_ref[:, ncols], c_ref[:, ncols],
                             dt_rows, a_rows, st_ref[gi], d_ref[:, cols])
            dx, db, dc, ddt, da, dhin, dd = vjp((dy_ref[:, cols], dh_scr[gi]))
            if has_acc:
                dx, db, dc = dx + acc_refs[0][:, cols], db + acc_refs[1][:, ncols], dc + acc_refs[2][:, ncols]
            dx_ref[:, cols] = dx
            db_ref[:, ncols] = db
            dc_ref[:, ncols] = dc
            for j in range(hg):
                ddt_ref[gi, pl.ds(j, 1), :] = ddt[j]
                dcum_ref[gi, pl.ds(j, 1), :] = da[j]
            dd_ref[:, cols] += dd
            dh_scr[gi] = dhin

    row_specs = [pl.BlockSpec((q, gpb * gw), lambda g, t: (order(t), g)),
                 pl.BlockSpec((q, gpb * n), lambda g, t: (order(t), g)),
                 pl.BlockSpec((q, gpb * n), lambda g, t: (order(t), g))]
    in_specs = row_specs + [
        pl.BlockSpec((LANES, q), lambda g, t: (0, order(t))),
        pl.BlockSpec((LANES, q), lambda g, t: (0, order(t))),
        pl.BlockSpec((1, gpb * gw), lambda g, t: (0, g)),
        pl.BlockSpec((None, gpb, n, gw), lambda g, t: (order(t), g, 0, 0)),
        pl.BlockSpec((q, gpb * gw), lambda g, t: (order(t), g))]
    args = [xs, bs, cs, dtt, cumt, dvec, st, dy]
    aliases = {}
    if has_acc:
        in_specs += row_specs
        args += list(acc)
        aliases = {8: 0, 9: 1, 10: 2}
    return pl.pallas_call(
        body, grid=(G // gpb, nct), in_specs=in_specs,
        out_specs=row_specs + [pl.BlockSpec((gpb, hgp, q), lambda g, t: (g, 0, order(t))),
                               pl.BlockSpec((gpb, hgp, q), lambda g, t: (g, 0, order(t))),
                               pl.BlockSpec((1, gpb * gw), lambda g, t: (0, g))],
        out_shape=[SDS(xs.shape, F32), SDS(bs.shape, F32), SDS(cs.shape, F32),
                   SDS((G, hgp, R), F32), SDS((G, hgp, R), F32), SDS((1, DI), F32)],
        scratch_shapes=[pltpu.VMEM((gpb, n, gw), F32)], input_output_aliases=aliases,
        name=name, compiler_params=_cp(("parallel", "arbitrary")),
    )(*args)


def _row_tile(width):
    return 256 if width <= 2048 else 128


def _rowwise_op(name, f, n_rows, out_widths, out_dtype=F32):
    def fwd(*args):
        rows, vecs = args[:n_rows], args[n_rows:]
        R = rows[0].shape[0]
        tm = min(R, _row_tile(max([a.shape[1] for a in rows] + list(out_widths))))
        outs = _rowwise(name, lambda r0, rt, vt: (f(r0, rt, vt), []), R, tm,
                        [(a, a.shape[1], 0) for a in rows], list(vecs),
                        [(w, out_dtype, None, 0, None) for w in out_widths])
        return (outs[0] if len(outs) == 1 else tuple(outs)), args

    def bwd(args, ct):
        rows, vecs = args[:n_rows], args[n_rows:]
        cts = [ct] if len(out_widths) == 1 else list(ct)
        R = rows[0].shape[0]
        tm = min(R, _row_tile(max([a.shape[1] for a in rows] + list(out_widths))))
        fb = _vjp_fn(f, n_rows, [True] * n_rows, [True] * len(vecs))
        res = _rowwise(name + "_bwd", fb, R, tm, [(a, a.shape[1], 0) for a in list(rows) + cts], list(vecs),
                       [(a.shape[1], F32, None, 0, None) for a in rows], [v.shape for v in vecs])
        return tuple(res)

    return _cvjp(fwd, bwd)


def _norm_mod_op(name, lc):
    def f(row0, rows, vecs):
        (h,), (g, shc, scc, shl, scl) = rows, vecs
        ctx = _is_ctx(row0, h.shape[0], lc)
        return [_rms(h, g) * (1.0 + jnp.where(ctx, scc, scl)) + jnp.where(ctx, shc, shl)]

    def fwd(h, *vecs):
        R, D = h.shape
        (u,) = _rowwise(name, lambda r0, rt, vt: (f(r0, rt, vt), []), R, min(R, _row_tile(D)),
                        [(h, D, 0)], list(vecs), [(D, BF16, None, 0, None)])
        return (u, h), (h,) + vecs

    def bwd(res, ct):
        h, vecs = res[0], res[1:]
        du, dhp = ct
        R, D = h.shape
        fb = _vjp_fn(f, 1, [True], [True] * 5, add_last=True)
        out = _rowwise(name + "_bwd", fb, R, min(R, _row_tile(D)), [(h, D, 0), (du, D, 0), (dhp, D, 0)], list(vecs),
                       [(D, F32, None, 0, None)], [v.shape for v in vecs])
        return tuple(out)

    return _cvjp(fwd, bwd)


def _post_res_op(name, lc):
    def branch(row0, rows, vecs):
        (y,), (g, gc, gl, bias) = rows, vecs
        return [jnp.where(_is_ctx(row0, y.shape[0], lc), gc, gl) * _rms(y + bias, g)]

    def full(row0, rows, vecs):
        return [rows[0] + branch(row0, rows[1:], vecs)[0]]

    def fwd(h, y, *vecs):
        R, D = h.shape
        (out,) = _rowwise(name, lambda r0, rt, vt: (full(r0, rt, vt), []), R, min(R, _row_tile(D)),
                          [(h, D, 0), (y, D, 0)], list(vecs), [(D, F32, None, 0, None)])
        return out, (y,) + vecs

    def bwd(res, d):
        y, vecs = res[0], res[1:]
        R, D = y.shape
        fb = _vjp_fn(branch, 1, [True], [True] * 4)
        out = _rowwise(name + "_bwd", fb, R, min(R, _row_tile(D)), [(y, D, 0), (d, D, 0)], list(vecs),
                       [(D, F32, None, 0, None)], [v.shape for v in vecs])
        return (d,) + tuple(out)

    return _cvjp(fwd, bwd)


def _payload(w, wd):
    layout, (k, ns), _ = wd
    wb = w.astype(BF16)
    return wb.reshape(k * ns // LANES, LANES) if layout == "flat" else wb


def _as_b3(g, wd):
    layout, (k, ns), npad = wd
    if layout == "col":
        return g
    if layout == "row":
        return g.reshape(1, N_DEV * k, ns)
    full = jnp.transpose(g.reshape(N_DEV, k, ns), (1, 0, 2)).reshape(k, N_DEV * ns)
    return jnp.pad(full, ((0, 0), (0, npad - N_DEV * ns)))[None]


def _as_blocks(dw3, wd):
    layout, (k, ns), _ = wd
    if layout == "col":
        return dw3
    if layout == "row":
        return dw3.reshape(N_DEV, k, ns)
    p = dw3[0, :, :N_DEV * ns].reshape(k, N_DEV, ns)
    return jnp.transpose(p, (1, 0, 2)).reshape(N_DEV, k * ns // LANES, LANES)


def _n_shards(wd):
    return N_DEV if wd[0] == "col" else 1


def _head_op(name, wd):
    def fwd(w):
        return _as_b3(_all_gather(name, _payload(w, wd)), wd), ()

    def bwd(_, db):
        return (_reduce_scatter(name + "_rs", _as_blocks(db, wd)).reshape(wd[1]),)

    return _cvjp(fwd, bwd)


def _chain_op(name, wd_cur, wd_next, next_exchanged):
    def fwd(a, b, wn):
        out, g = _matmul(name, "nn", a, b, rider=("gather", _payload(wn, wd_next)))
        return (out, _as_b3(g, wd_next)), (a, b)

    def bwd(res, ct):
        a, b = res
        dout, dbn = ct
        if next_exchanged:
            da, recv = _matmul(name + "_da_plain", "nt", dout, b, out_dtype=a.dtype), dbn
        else:
            da, recv = _matmul(name + "_da", "nt", dout, b, out_dtype=a.dtype,
                               rider=("exchange", _as_blocks(dbn, wd_next)))
        dw = _matmul(name + "_dw", "tn", a, dout, S=_n_shards(wd_cur), out_dtype=BF16)
        return da, dw, _sum_blocks(name + "_rsum", recv).reshape(wd_next[1])

    return _cvjp(fwd, bwd)


def _mlp_op(name, wd2, wd_next):
    has_next = wd_next is not None

    def fwd(u, b1, w2, *wn):
        act, g2 = _matmul(name + "_up", "nn", u, b1, out_dtype=BF16,
                          epilogue=lambda acc: jnp.square(jnp.maximum(acc, 0.0)), rider=("gather", _payload(w2, wd2)))
        b2 = _as_b3(g2, wd2)
        if has_next:
            f, gn = _matmul(name + "_down", "nn", act, b2, rider=("gather", _payload(wn[0], wd_next)))
            return (f, _as_b3(gn, wd_next)), (u, act, b1, b2)
        return _matmul(name + "_down_last", "nn", act, b2), (u, act, b1, b2)

    def bwd(res, ct):
        u, act, b1, b2 = res
        dact = dict(out_dtype=BF16, extra=act, epilogue=lambda acc, a: acc * (2.0 * jnp.sqrt(a.astype(F32))))
        if has_next:
            df, dbn = ct
            da, recvn = _matmul(name + "_dact", "nt", df, b2, rider=("exchange", _as_blocks(dbn, wd_next)), **dact)
        else:
            df = ct
            da = _matmul(name + "_dact_last", "nt", df, b2, **dact)
        dw2 = _matmul(name + "_dw2", "tn", act, df, S=1, out_dtype=BF16)
        dw1, recv2 = _matmul(name + "_dw1", "tn", u, da, S=N_DEV, out_dtype=BF16,
                             rider=("exchange", _as_blocks(dw2, wd2)))
        du, recv1 = _matmul(name + "_du", "nt", da, b1, out_dtype=u.dtype, rider=("exchange", dw1))
        out = (du, recv1, _sum_blocks(name + "_rsum2", recv2).reshape(wd2[1]))
        if has_next:
            out += (_sum_blocks(name + "_rsumn", recvn).reshape(wd_next[1]),)
        return out

    return _cvjp(fwd, bwd)


def _conv_op(name, lc, seg_c, seg_l):
    def fwd(v, w, b):
        return _conv_fwd(name, v, 0, v.shape[1], w, b, lc, seg_c, seg_l), (v, w)

    def bwd(res, dy):
        v, w = res
        C = v.shape[1]
        dv = _conv_fwd(name + "_dx", dy, 0, C, w[::-1], jnp.zeros((C,), F32), lc, seg_c, seg_l)
        dw, db = _conv_bwd_w(name + "_dw", v, 0, dy, w.shape, lc, seg_c, seg_l)
        return dv, dw, db

    return _cvjp(fwd, bwd)


def _ssm_core_op(name, cfg):
    DI, GN, H, lc, L, NP = cfg["DI"], cfg["GN"], cfg["H"], cfg["Lc"], cfg["L"], cfg["NP"]
    parts = [(DI, DI), (2 * DI, GN), (2 * DI + GN, GN)]
    dt_col = 2 * DI + 2 * GN

    def gated(row0, rows, vecs):
        (yf, yb, z), (g,) = rows, vecs
        return [_rms((yf + yb) * _silu(z), g)]

    def fwd(proj, cw, cb, alog, bias, d_f, d_b, ng):
        R = proj.shape[0]
        pre, off = [], 0
        for idx, (col, wd) in enumerate(parts):
            pre.append(_conv_fwd(f"{name}_conv{idx}", proj, col, wd, cw[:, off:off + wd], cb[off:off + wd], lc, lc, L))
            off += wd
        dtt, cumt = _dt_prep(name + "_dt", proj, dt_col // LANES, bias, alog, H)
        yf, stf = _ssd_fwd(name + "_scan_f", False, *pre, dtt, cumt, d_f, H, lc)
        yb, stb = _ssd_fwd(name + "_scan_b", True, *pre, dtt, cumt, d_b, H, lc)
        tm = min(R, _row_tile(DI))
        (out,) = _rowwise(name + "_gate", lambda r0, rt, vt: (gated(r0, rt, vt), []), R, tm,
                          [(yf, DI, 0), (yb, DI, 0), (proj, DI, 0)], [ng], [(DI, BF16, None, 0, None)])
        return out, (proj, cw, alog, bias, d_f, d_b, ng, pre, dtt, cumt, yf, yb, stf, stb)

    def bwd(res, dout):
        proj, cw, alog, bias, d_f, d_b, ng, pre, dtt, cumt, yf, yb, stf, stb = res
        R = proj.shape[0]
        tm = min(R, _row_tile(DI))
        fb = _vjp_fn(gated, 3, [True, False, True], [True])
        dy, dproj, dng = _rowwise(name + "_gate_bwd", fb, R, tm,
                                  [(yf, DI, 0), (yb, DI, 0), (proj, DI, 0), (dout, DI, 0)], [ng],
                                  [(DI, F32, None, 0, None), (DI, F32, NP, 0, None)], [ng.shape])
        rf = _ssd_bwd(name + "_scan_f_bwd", False, *pre, dtt, cumt, d_f, stf, dy, H, lc)
        rb = _ssd_bwd(name + "_scan_b_bwd", True, *pre, dtt, cumt, d_b, stb, dy, H, lc, acc=rf[:3])
        hg = DI // N_GROUPS // HEAD_P

        def head_rows(t3):
            return t3[:, :hg].reshape(H, R)

        zpad = jnp.zeros((LANES - 2 * H, R), F32)
        ddt = jnp.concatenate([head_rows(rf[3]), head_rows(rb[3]), zpad], axis=0)
        dcum = jnp.concatenate([head_rows(rf[4]), head_rows(rb[4]), zpad], axis=0)
        dproj, dbias, dalog = _dt_prep_bwd(name + "_dt_bwd", proj, dt_col // LANES, bias, alog, H, ddt, dcum,
                                           dproj, dt_col // (4 * LANES))
        dws, dbs, off = [], [], 0
        for idx, (col, wd) in enumerate(parts):
            dpart = rb[idx]
            dproj = _conv_fwd(f"{name}_conv{idx}_dx", dpart, 0, wd, cw[::-1, off:off + wd], jnp.zeros((wd,), F32),
                              lc, lc, L, out=dproj, ocol0=col)
            dw, db = _conv_bwd_w(f"{name}_conv{idx}_dw", proj, col, dpart, (cw.shape[0], wd), lc, lc, L)
            dws.append(dw)
            dbs.append(db)
            off += wd
        return (dproj, jnp.concatenate(dws, axis=1), jnp.concatenate(dbs), dalog, dbias, rf[5], rb[5], dng)

    return _cvjp(fwd, bwd)


def _mod_op(name):
    def fwd(s8, w, b):
        depth, d, ns = w.shape
        me = _lin(_my_pos())
        rows = N_DEV * SUBLANES
        s_all = _all_gather(name + "_s", s8).reshape(rows, d)
        wb = w.astype(BF16)
        bcols = lax.dynamic_slice_in_dim(b, me * ns, ns, axis=1)
        part = jnp.concatenate([_matmul(name + "_mm", "nn", s_all, wb[i][None], bias=bcols[i][None])
                                for i in range(depth)], axis=1)
        allm = _all_gather(name + "_m", part)
        mine = lax.dynamic_slice_in_dim(allm, me * SUBLANES, SUBLANES, axis=1)
        m = jnp.transpose(mine.reshape(N_DEV, SUBLANES, depth, ns), (2, 1, 0, 3)).reshape(depth, SUBLANES, N_DEV * ns)
        return m, (s_all, wb)

    def bwd(res, dm):
        s_all, wb = res
        depth, d, ns = wb.shape
        me = _lin(_my_pos())
        rows = N_DEV * SUBLANES
        fac = jnp.transpose(dm, (1, 0, 2)).reshape(SUBLANES, depth * N_DEV * ns)
        allf = _all_gather(name + "_fac", fac).reshape(rows, depth, N_DEV, ns)
        mine = lax.dynamic_slice_in_dim(allf, me, 1, axis=2)[:, :, 0]
        dws = [_matmul(name + "_dw", "tn", s_all, mine[:, i], S=1, out_dtype=F32)[0] for i in range(depth)]
        ds_part = sum(_matmul(name + "_ds", "nt", mine[:, i], wb[i][None]) for i in range(depth))
        ds_all = _all_reduce(name + "_dsr", ds_part)
        ds = lax.dynamic_slice_in_dim(ds_all, me * SUBLANES, SUBLANES, axis=0)
        db = _sum_blocks(name + "_db", allf.reshape(rows, depth * N_DEV * ns // LANES, LANES))
        return ds, jnp.stack(dws), db.reshape(depth, N_DEV * ns)

    return _cvjp(fwd, bwd)


def _loss_grad(name, h, target, lc):
    R, D = h.shape
    tm = min(lc, 256)
    off = lc // tm

    def body(h_ref, t_ref, g_ref, s_ref):
        i = pl.program_id(0)

        @pl.when(i == 0)
        def _():
            s_ref[...] = jnp.zeros_like(s_ref)

        err = jnp.where(i >= off, h_ref[...] - t_ref[...], 0.0)
        g_ref[...] = err * (1.0 / D)
        s_ref[...] += jnp.sum(err * err, axis=0, keepdims=True)

    return pl.pallas_call(
        body, grid=(R // tm,),
        in_specs=[pl.BlockSpec((tm, D), lambda i: (i, 0)),
                  pl.BlockSpec((tm, D), lambda i: (jnp.maximum(i - off, 0), 0))],
        out_specs=[pl.BlockSpec((tm, D), lambda i: (i, 0)), pl.BlockSpec((1, D), lambda i: (0, 0))],
        out_shape=[SDS((R, D), F32), SDS((1, D), F32)], name=name, compiler_params=_cp(("arbitrary",)),
    )(h, target)


def _adam(name, w, g, m, v):
    shape = w.shape
    c = shape[-1]
    r = w.size // c
    tr = _pick(r, (256, 128, 64, 32, 16, 8))
    tc = _pick(c, (2048, 1536, 1024, 768, 512, 256, 128))
    c1 = 1.0 / (1.0 - ADAM_B1 ** ADAM_STEP)
    c2 = 1.0 / (1.0 - ADAM_B2 ** ADAM_STEP)

    def body(w_ref, g_ref, m_ref, v_ref, d_ref, mo_ref, vo_ref):
        gg = g_ref[...]
        mn = ADAM_B1 * m_ref[...] + (1.0 - ADAM_B1) * gg
        vn = ADAM_B2 * v_ref[...] + (1.0 - ADAM_B2) * jnp.square(gg)
        d_ref[...] = -ADAM_LR * ((mn * c1) / (jnp.sqrt(vn * c2) + ADAM_EPS) + ADAM_WD * w_ref[...])
        mo_ref[...] = mn
        vo_ref[...] = vn

    spec = pl.BlockSpec((tr, tc), lambda i, j: (i, j))
    res = pl.pallas_call(
        body, grid=(r // tr, c // tc), in_specs=[spec] * 4, out_specs=[spec] * 3,
        out_shape=[SDS((r, c), F32)] * 3, name=name, compiler_params=_cp(("parallel", "parallel")),
    )(*[a.reshape(r, c) for a in (w, g, m, v)])
    return [a.reshape(shape) for a in res]


WEIGHTS = ["c_ctx", "mod_w", "mod_b", "pre_mix_g", "post_mix_g", "pre_mlp_g", "post_mlp_g", "mlp_w1", "mlp_w2",
           "ssm_in_w", "ssm_conv_w", "ssm_conv_b", "ssm_a_log_f", "ssm_dt_bias_f", "ssm_d_f", "ssm_a_log_b",
           "ssm_dt_bias_b", "ssm_d_b", "ssm_norm_g", "ssm_out_w", "conf_pw1_w", "conf_pw1_b", "conf_dw_w",
           "conf_dw_b", "conf_ln_g", "conf_ln_b", "conf_pw2_w", "conf_pw2_b"]
BIG_SHARDED = ["mod_w", "mlp_w1", "mlp_w2", "ssm_in_w", "ssm_out_w", "conf_pw1_w", "conf_pw2_w"]
SMALL_SHARDED = {"ssm_conv_w": 2, "conf_pw1_b": 1, "conf_dw_w": 2, "conf_dw_b": 1, "conf_ln_g": 1, "conf_ln_b": 1,
                 "conf_pw2_b": 1}


def _pack(arrs):
    flat = jnp.concatenate([a.reshape(-1).astype(F32) for a in arrs])
    n = flat.shape[0]
    tile = LANES * SUBLANES
    npad = -(-n // tile) * tile
    return jnp.pad(flat, (0, npad - n)).reshape(npad // LANES, LANES)


def _unpack(flat, shapes):
    out, off = [], 0
    for s in shapes:
        n = 1
        for d in s:
            n *= d
        out.append(flat[off:off + n].reshape(s))
        off += n
    return out


def _to_col(h, lc, rows_g):
    d = h.shape[1]
    lat = h[lc:].reshape(rows_g, GRID_W, d).swapaxes(0, 1).reshape(-1, d)
    return jnp.concatenate([h[:lc], lat], axis=0)


def _from_col(h, lc, rows_g):
    d = h.shape[1]
    lat = h[lc:].reshape(GRID_W, rows_g, d).swapaxes(0, 1).reshape(-1, d)
    return jnp.concatenate([h[:lc], lat], axis=0)


def _step(a):
    x, ctx, target = a["x"][0], a["ctx"][0], a["loss_target"][0]
    L, D = x.shape
    lc = ctx.shape[0]
    depth = a["mod_w"].shape[0]
    DI = 2 * D
    H = DI // HEAD_P
    GN = N_GROUPS * N_STATE
    cfg = dict(DI=DI, GN=GN, H=H, Lc=lc, L=L, NP=2 * DI + 2 * GN + 4 * LANES)
    rows_g = L // GRID_W
    me = _lin(_my_pos())

    names = list(SMALL_SHARDED)
    gathered = _all_gather("small_w_ag", _pack([a[n] for n in names])).reshape(N_DEV, -1)
    full_small, off = {}, 0
    for n in names:
        shp, ax = a[n].shape, SMALL_SHARDED[n]
        seg = gathered[:, off:off + a[n].size].reshape((N_DEV,) + shp)
        off += a[n].size
        full_small[n] = jnp.moveaxis(seg, 0, ax).reshape(shp[:ax] + (N_DEV * shp[ax],) + shp[ax + 1:])

    params = {n: (full_small[n] if n in SMALL_SHARDED else a[n]) for n in WEIGHTS}
    params["x"] = x

    def trunk(p):
        raw8 = jnp.concatenate([a["c"], p["c_ctx"][None], jnp.zeros((SUBLANES - 2, D), F32)], axis=0)
        s8 = _rowwise_op("silu_c", lambda r0, rows, vecs: [_silu(rows[0])], 1, [D])(raw8)
        mods = _mod_op("mod")(s8, p["mod_w"], p["mod_b"])
        zero_bias = jnp.zeros((1, D), F32)
        h = jnp.concatenate([ctx, p["x"]], axis=0)

        seq = []
        for i in range(depth):
            j = i // 2
            mixer = [("in", "ssm_in_w", "flat"), ("out", "ssm_out_w", "row")] if i % 2 == 0 else \
                    [("pw1", "conf_pw1_w", "col"), ("pw2", "conf_pw2_w", "row")]
            for role, n, layout in mixer:
                seq.append((role, p[n][j], (layout, p[n][j].shape, cfg["NP"])))
            seq.append(("w1", p["mlp_w1"][i], ("col", p["mlp_w1"][i].shape, 0)))
            seq.append(("w2", p["mlp_w2"][i], ("row", p["mlp_w2"][i].shape, 0)))
        state = {"t": 0, "b": _head_op("head_ag", seq[0][2])(seq[0][1])}

        def mm(a):
            t = state["t"]
            out, state["b"] = _chain_op(seq[t][0], seq[t][2], seq[t + 1][2], seq[t + 1][0] == "w1")(
                a, state["b"], seq[t + 1][1])
            state["t"] = t + 1
            return out

        def mlp(u2):
            t = state["t"]
            if t + 2 < len(seq):
                f, state["b"] = _mlp_op("mlp_" + seq[t + 2][0], seq[t + 1][2], seq[t + 2][2])(
                    u2, state["b"], seq[t + 1][1], seq[t + 2][1])
            else:
                f = _mlp_op("mlp", seq[t + 1][2], None)(u2, state["b"], seq[t + 1][1])
            state["t"] = t + 2
            return f

        for i in range(depth):
            kind, j = i % 2, i // 2
            col_major = (j % 2) == 1
            if i == 2:
                h = _to_col(h, lc, rows_g)
            sh1, sc1, g1, sh2, sc2, g2 = jnp.split(mods[i], 6, axis=1)
            u, hp = _norm_mod_op("pre_norm", lc)(h, p["pre_mix_g"][i][None], sh1[1:2], sc1[1:2], sh1[0:1], sc1[0:1])
            if kind == 0:
                proj = mm(u)
                pad = jnp.zeros((LANES - 2 * H,), F32)
                alog = jnp.concatenate([p["ssm_a_log_f"][j], p["ssm_a_log_b"][j], pad])[None]
                bias = jnp.concatenate([p["ssm_dt_bias_f"][j], p["ssm_dt_bias_b"][j], pad])[None]
                y = _ssm_core_op("ssm", cfg)(
                    proj, p["ssm_conv_w"][j], p["ssm_conv_b"][j], alog, bias,
                    jnp.repeat(p["ssm_d_f"][j], HEAD_P)[None], jnp.repeat(p["ssm_d_b"][j], HEAD_P)[None],
                    p["ssm_norm_g"][j][None])
                o = mm(y)
                mix_bias = zero_bias
            else:
                pre = mm(u)
                glu = _rowwise_op("glu", lambda r0, rows, vecs: [
                    (rows[0] + vecs[0])[:, :D] * jax.nn.sigmoid((rows[0] + vecs[0])[:, D:])], 1, [D])
                v = glu(pre, p["conf_pw1_b"][j][None])
                seg = rows_g if col_major else GRID_W
                cv = _conv_op(f"dw{seg}", lc, lc, seg)(v, p["conf_dw_w"][j], p["conf_dw_b"][j])

                def ln_swish(r0, rows, vecs):
                    xc = rows[0] - jnp.mean(rows[0], axis=-1, keepdims=True)
                    yn = xc * lax.rsqrt(jnp.mean(xc * xc, axis=-1, keepdims=True) + EPS) * vecs[0] + vecs[1]
                    return [_silu(yn)]

                w = _rowwise_op("ln_swish", ln_swish, 1, [D], BF16)(cv, p["conf_ln_g"][j][None], p["conf_ln_b"][j][None])
                o = mm(w)
                mix_bias = p["conf_pw2_b"][j][None]
            h = _post_res_op("post_norm", lc)(hp, o, p["post_mix_g"][i][None], g1[1:2], g1[0:1], mix_bias)
            u2, hp = _norm_mod_op("pre_norm", lc)(h, p["pre_mlp_g"][i][None], sh2[1:2], sc2[1:2], sh2[0:1], sc2[0:1])
            f = mlp(u2)
            h = _post_res_op("post_norm", lc)(hp, f, p["post_mlp_g"][i][None], g2[1:2], g2[0:1], zero_bias)
        return _from_col(h, lc, rows_g) if depth > 2 else h

    h_out, vjp = jax.vjp(trunk, params)
    dh, sq = _loss_grad("loss", h_out, target, lc)
    (grads,) = vjp(dh)
    loss = lax.psum(jnp.sum(sq) * (0.5 / D), ("x", "y", "c"))

    part_names = [n for n in WEIGHTS if n not in BIG_SHARDED and n != "mod_b"]
    red = _all_reduce("small_g", _pack([grads[n] for n in part_names])).reshape(-1)
    reduced = dict(zip(part_names, _unpack(red, [grads[n].shape for n in part_names])))
    final = {}
    for n in WEIGHTS:
        if n in BIG_SHARDED or n == "mod_b":
            final[n] = grads[n]
        elif n in SMALL_SHARDED:
            ax = SMALL_SHARDED[n]
            final[n] = lax.dynamic_slice_in_dim(reduced[n], me * a[n].shape[ax], a[n].shape[ax], axis=ax)
        else:
            final[n] = reduced[n]

    deltas, new_m, new_v = [], [], []
    for n in WEIGHTS:
        d_, m_, v_ = _adam("adam_" + n, a[n], final[n], a["m_" + n], a["v_" + n])
        deltas.append(d_)
        new_m.append(m_)
        new_v.append(v_)
    return (loss, grads["x"][None], *[final[n] for n in WEIGHTS], *deltas, *new_m, *new_v)


def kernel(x, c, ctx, c_ctx, mod_w, mod_b, pre_mix_g, post_mix_g, pre_mlp_g, post_mlp_g, mlp_w1, mlp_w2, ssm_in_w, ssm_conv_w, ssm_conv_b, ssm_a_log_f, ssm_dt_bias_f, ssm_d_f, ssm_a_log_b, ssm_dt_bias_b, ssm_d_b, ssm_norm_g, ssm_out_w, conf_pw1_w, conf_pw1_b, conf_dw_w, conf_dw_b, conf_ln_g, conf_ln_b, conf_pw2_w, conf_pw2_b, loss_target, m_c_ctx, m_mod_w, m_mod_b, m_pre_mix_g, m_post_mix_g, m_pre_mlp_g, m_post_mlp_g, m_mlp_w1, m_mlp_w2, m_ssm_in_w, m_ssm_conv_w, m_ssm_conv_b, m_ssm_a_log_f, m_ssm_dt_bias_f, m_ssm_d_f, m_ssm_a_log_b, m_ssm_dt_bias_b, m_ssm_d_b, m_ssm_norm_g, m_ssm_out_w, m_conf_pw1_w, m_conf_pw1_b, m_conf_dw_w, m_conf_dw_b, m_conf_ln_g, m_conf_ln_b, m_conf_pw2_w, m_conf_pw2_b, v_c_ctx, v_mod_w, v_mod_b, v_pre_mix_g, v_post_mix_g, v_pre_mlp_g, v_post_mlp_g, v_mlp_w1, v_mlp_w2, v_ssm_in_w, v_ssm_conv_w, v_ssm_conv_b, v_ssm_a_log_f, v_ssm_dt_bias_f, v_ssm_d_f, v_ssm_a_log_b, v_ssm_dt_bias_b, v_ssm_d_b, v_ssm_norm_g, v_ssm_out_w, v_conf_pw1_w, v_conf_pw1_b, v_conf_dw_w, v_conf_dw_b, v_conf_ln_g, v_conf_ln_b, v_conf_pw2_w, v_conf_pw2_b):
    return _step(dict(locals()))
```
